```python
import math
import jax, jax.numpy as jnp
from jax import lax
import numpy as np

D_MODEL = 1024
BATCH = 8
SEQ = 4096
DEPTH = 1

HEAD_DIM = 64
N_HEADS = D_MODEL // HEAD_DIM
A_Q_HEADS = N_HEADS // 2
A_KV_HEADS = 2
A_GROUP = A_Q_HEADS // A_KV_HEADS
A_MAX_DIST = 127
B_HEADS = N_HEADS - A_Q_HEADS
B_PATTERNS = ((128, 1), (512, 4), (2048, 16))
BLOCK = 128
D_FF = 2816
CONV_WIDTH = 3
ALPHA = (2.0 * DEPTH) ** 0.25
BETA = (8.0 * DEPTH) ** -0.25
LN_EPS = 1e-5
RMS_EPS = 1e-6

A_Q_W = A_Q_HEADS * HEAD_DIM
A_KV_W = A_KV_HEADS * HEAD_DIM
B_W = B_HEADS * HEAD_DIM
IN_SPLITS = (A_Q_W, A_KV_W, A_KV_W, B_W, B_W, B_W)
IN_W = sum(IN_SPLITS)

kernel_name = "hymba_swa_sink_dilated_convffn_deepnorm"


def alibi_slopes(n):
    return jnp.asarray(np.array([2.0 ** (-8.0 * (i + 1) / n) for i in range(n)], dtype=np.float32))


def layer_norm(x, g, b):
    xf = x.astype(jnp.float32)
    mu = jnp.mean(xf, -1, keepdims=True)
    var = jnp.mean(jnp.square(xf - mu), -1, keepdims=True)
    return ((xf - mu) * lax.rsqrt(var + LN_EPS) * g.astype(jnp.float32) + b.astype(jnp.float32)).astype(x.dtype)


def rms_norm(x, g):
    xf = x.astype(jnp.float32)
    return (xf * lax.rsqrt(jnp.mean(jnp.square(xf), -1, keepdims=True) + RMS_EPS) * g.astype(jnp.float32)).astype(x.dtype)


def banded_attention(q, k, v, slope, max_dist, dist_unit, sink=None):
    L, dh = q.shape[-2], q.shape[-1]
    nb = -(-L // BLOCK)
    pad = nb * BLOCK - L
    q = jnp.pad(q, [(0, 0)] * (q.ndim - 2) + [(0, pad), (0, 0)])
    kv_pad = [(0, 0)] * (k.ndim - 2) + [(BLOCK, pad), (0, 0)]
    k = jnp.pad(k, kv_pad)
    v = jnp.pad(v, kv_pad)
    lead = k.shape[:-2]
    kb = k.reshape(*lead, nb + 1, BLOCK, dh)
    vb = v.reshape(*lead, nb + 1, BLOCK, dh)
    kw = jnp.concatenate([kb[..., :-1, :, :], kb[..., 1:, :, :]], axis=-2)
    vw = jnp.concatenate([vb[..., :-1, :, :], vb[..., 1:, :, :]], axis=-2)
    qb = q.reshape(*q.shape[:-2], nb, BLOCK, dh)
    s = jnp.einsum('...gnqd,...nkd->...gnqk', qb, kw,
                   preferred_element_type=jnp.float32) * (1.0 / math.sqrt(dh))
    qi = jnp.arange(BLOCK)[:, None]
    ki = jnp.arange(2 * BLOCK)[None, :]
    dist = BLOCK + qi - ki
    key_pos = jnp.arange(nb)[:, None, None] * BLOCK - BLOCK + ki[None]
    mask = (dist >= 0)[None] & (dist <= max_dist)[None] & (key_pos >= 0)
    s = s - slope.astype(jnp.float32) * (dist * dist_unit).astype(jnp.float32)
    s = jnp.where(mask, s, -jnp.inf)
    m = jnp.max(s, -1, keepdims=True)
    if sink is not None:
        sk = sink.astype(jnp.float32)
        m = jnp.maximum(m, sk)
        p = jnp.exp(s - m)
        l = jnp.sum(p, -1, keepdims=True) + jnp.exp(sk - m)
    else:
        p = jnp.exp(s - m)
        l = jnp.sum(p, -1, keepdims=True)
    o = jnp.einsum('...gnqk,...nkd->...gnqd', p, vw.astype(jnp.float32)) / l
    lse = (m + jnp.log(l))[..., 0]
    o = o.reshape(*o.shape[:-3], nb * BLOCK, dh)[..., :L, :].astype(q.dtype)
    lse = lse.reshape(*lse.shape[:-2], nb * BLOCK)[..., :L]
    return o, lse


def mixer_a(qa, ka, va, sinks):
    Bn, S, _ = qa.shape
    q = qa.reshape(Bn, S, A_KV_HEADS, A_GROUP, HEAD_DIM).transpose(0, 2, 3, 1, 4)
    k = ka.reshape(Bn, S, A_KV_HEADS, HEAD_DIM).transpose(0, 2, 1, 3)
    v = va.reshape(Bn, S, A_KV_HEADS, HEAD_DIM).transpose(0, 2, 1, 3)
    slope = alibi_slopes(A_Q_HEADS).reshape(A_KV_HEADS, A_GROUP, 1, 1, 1)
    sink = sinks.reshape(A_KV_HEADS, A_GROUP, 1, 1, 1)
    o, _ = banded_attention(q, k, v, slope, A_MAX_DIST, 1, sink)
    return o.transpose(0, 3, 1, 2, 4).reshape(Bn, S, A_Q_W)


def mixer_b(qb, kb, vb):
    Bn, S, _ = qb.shape
    q = qb.reshape(Bn, S, B_HEADS, HEAD_DIM).transpose(0, 2, 1, 3)
    k = kb.reshape(Bn, S, B_HEADS, HEAD_DIM).transpose(0, 2, 1, 3)
    v = vb.reshape(Bn, S, B_HEADS, HEAD_DIM).transpose(0, 2, 1, 3)
    slope = alibi_slopes(B_HEADS).reshape(B_HEADS, 1, 1, 1, 1, 1)
    outs, lses = [], []
    for (w, r) in B_PATTERNS:
        Lr = S // r
        qr = q.reshape(Bn, B_HEADS, Lr, r, HEAD_DIM).swapaxes(2, 3)[:, :, :, None]
        kr = k.reshape(Bn, B_HEADS, Lr, r, HEAD_DIM).swapaxes(2, 3)
        vr = v.reshape(Bn, B_HEADS, Lr, r, HEAD_DIM).swapaxes(2, 3)
        o, lse = banded_attention(qr, kr, vr, slope, w // r, r)
        outs.append(o[:, :, :, 0].swapaxes(2, 3).reshape(Bn, B_HEADS, S, HEAD_DIM))
        lses.append(lse[:, :, :, 0].swapaxes(2, 3).reshape(Bn, B_HEADS, S))
    wts = jax.nn.softmax(jnp.stack(lses, 0), axis=0)
    o = jnp.sum(wts[..., None] * jnp.stack(outs, 0).astype(jnp.float32), axis=0)
    return o.astype(qb.dtype).transpose(0, 2, 1, 3).reshape(Bn, S, B_W)


def causal_dwconv(u, w, b):
    K = w.shape[0]
    S = u.shape[1]
    up = jnp.pad(u, ((0, 0), (K - 1, 0), (0, 0)))
    y = up[:, 0:S, :] * w[0]
    for j in range(1, K):
        y = y + up[:, j:j + S, :] * w[j]
    return y + b


def setup_inputs(seed: int = 0) -> dict:
    key = jax.random.key(seed)
    ks = jax.random.split(key, 16)
    f32 = jnp.float32
    x = jax.random.normal(ks[0], (BATCH, SEQ, D_MODEL), f32)
    col_scale = jnp.concatenate([
        jnp.ones((A_Q_W + A_KV_W,), f32), jnp.full((A_KV_W,), BETA, f32),
        jnp.ones((2 * B_W,), f32), jnp.full((B_W,), BETA, f32)])
    w_in = jax.random.normal(ks[1], (D_MODEL, IN_W), f32) * D_MODEL ** -0.5 * col_scale
    norm_a_g = 1.0 + 0.02 * jax.random.normal(ks[2], (A_Q_W,), f32)
    norm_b_g = 1.0 + 0.02 * jax.random.normal(ks[3], (B_W,), f32)
    sinks_a = 0.5 * jax.random.normal(ks[4], (A_Q_HEADS,), f32)
    w_o = jax.random.normal(ks[5], (D_MODEL, D_MODEL), f32) * D_MODEL ** -0.5 * BETA
    ln1_g = 1.0 + 0.02 * jax.random.normal(ks[6], (D_MODEL,), f32)
    ln1_b = 0.02 * jax.random.normal(ks[7], (D_MODEL,), f32)
    w_up = jax.random.normal(ks[8], (D_MODEL, 2 * D_FF), f32) * D_MODEL ** -0.5 * BETA
    conv_w = jax.random.normal(ks[9], (CONV_WIDTH, 2 * D_FF), f32) * CONV_WIDTH ** -0.5
    conv_b = 0.02 * jax.random.normal(ks[10], (2 * D_FF,), f32)
    w_down = jax.random.normal(ks[11], (D_FF, D_MODEL), f32) * D_FF ** -0.5 * BETA
    ln2_g = 1.0 + 0.02 * jax.random.normal(ks[12], (D_MODEL,), f32)
    ln2_b = 0.02 * jax.random.normal(ks[13], (D_MODEL,), f32)
    return {"x": x, "w_in": w_in, "norm_a_g": norm_a_g, "norm_b_g": norm_b_g,
            "sinks_a": sinks_a, "w_o": w_o, "ln1_g": ln1_g, "ln1_b": ln1_b,
            "w_up": w_up, "conv_w": conv_w, "conv_b": conv_b, "w_down": w_down,
            "ln2_g": ln2_g, "ln2_b": ln2_b}


def reference(x, w_in, norm_a_g, norm_b_g, sinks_a, w_o, ln1_g, ln1_b,
              w_up, conv_w, conv_b, w_down, ln2_g, ln2_b):
    h = x
    for _ in range(DEPTH):
        proj = h @ w_in
        offs = np.cumsum((0,) + IN_SPLITS)
        qa, ka, va, qb, kb, vb = [proj[..., offs[i]:offs[i + 1]] for i in range(len(IN_SPLITS))]
        oa = rms_norm(mixer_a(qa, ka, va, sinks_a), norm_a_g)
        ob = rms_norm(mixer_b(qb, kb, vb), norm_b_g)
        mix = jnp.concatenate([oa, ob], axis=-1) @ w_o
        h = layer_norm(ALPHA * h + mix, ln1_g, ln1_b)
        u = causal_dwconv(h @ w_up, conv_w, conv_b)
        gate, val = u[..., :D_FF], u[..., D_FF:]
        ff = (jax.nn.gelu(gate) * val) @ w_down
        h = layer_norm(ALPHA * h + ff, ln2_g, ln2_b)
    return h
```

```python
import functools
import math

import numpy as np
import jax
import jax.numpy as jnp
from jax import lax
from jax.experimental import pallas as pl
from jax.experimental.pallas import tpu as pltpu

HEAD_DIM = 64
LANES = 128
Q_HEADS = 8
QW = Q_HEADS * HEAD_DIM
A_KV_W = 128
BLOCK = 128
A_MAX_DIST = 127
B_PATTERNS = ((128, 1), (512, 4), (2048, 16))
DIL = 16
TI = 32
TM = DIL * TI
D_FF = 2816
FF_CHUNK = 256
N_FF_CHUNKS = D_FF // FF_CHUNK
HALO = 16
ALPHA = 2.0 ** 0.25
LN_EPS = 1e-5
RMS_EPS = 1e-6
VMEM_LIMIT = 48 * 1024 * 1024

_BF16 = jnp.bfloat16
_F32 = jnp.float32


def _alibi_slopes(n):
    return np.array([2.0 ** (-8.0 * (i + 1) / n) for i in range(n)], dtype=np.float32)


def _dot(a, b):
    return jnp.dot(a, b, preferred_element_type=_F32)


def _dot_nt(a, b):
    return lax.dot_general(a, b, (((1,), (1,)), ((), ())), preferred_element_type=_F32)


def _proj_body(x_ref, w_ref, qa_ref, ka_ref, va_ref, qb_ref, kb_ref, vb_ref,
               qbp_ref, kbp_ref, vbp_ref, slab_ref):
    xb = x_ref[...].astype(_BF16)
    ya = _dot(xb, w_ref[:, 0:QW + 2 * A_KV_W])
    qa_ref[...] = ya[:, 0:QW].astype(_BF16)
    ka_ref[...] = ya[:, QW:QW + A_KV_W].astype(_BF16)
    va_ref[...] = ya[:, QW + A_KV_W:QW + 2 * A_KV_W].astype(_BF16)
    base = QW + 2 * A_KV_W
    for idx, (nat_ref, perm_ref) in enumerate(((qb_ref, qbp_ref), (kb_ref, kbp_ref), (vb_ref, vbp_ref))):
        y = _dot(xb, w_ref[:, base + QW * idx:base + QW * (idx + 1)])
        nat_ref[...] = y.astype(_BF16)
        for j in range(QW // LANES):
            slab_ref[j] = y[:, j * LANES:(j + 1) * LANES]
        for j in range(QW // LANES):
            for c in range(DIL):
                perm_ref[c, :, j * LANES:(j + 1) * LANES] = (
                    slab_ref[j, pl.ds(c, TI, stride=DIL), :].astype(_BF16))


def _project(xm, w_in_p):
    m, d = xm.shape
    nt = m // TM
    nat = lambda w: jax.ShapeDtypeStruct((m, w), _BF16)
    perm = jax.ShapeDtypeStruct((nt, DIL, TI, QW), _BF16)
    nat_spec = lambda w: pl.BlockSpec((TM, w), lambda i: (i, 0))
    perm_spec = pl.BlockSpec((None, DIL, TI, QW), lambda i: (i, 0, 0, 0))
    return pl.pallas_call(
        _proj_body,
        grid=(nt,),
        in_specs=[pl.BlockSpec((TM, d), lambda i: (i, 0)),
                  pl.BlockSpec(w_in_p.shape, lambda i: (0, 0), pipeline_mode=pl.Buffered(1))],
        out_specs=[nat_spec(QW), nat_spec(A_KV_W), nat_spec(A_KV_W),
                   nat_spec(QW), nat_spec(QW), nat_spec(QW), perm_spec, perm_spec, perm_spec],
        out_shape=[nat(QW), nat(A_KV_W), nat(A_KV_W), nat(QW), nat(QW), nat(QW), perm, perm, perm],
        scratch_shapes=[pltpu.VMEM((QW // LANES, TM, LANES), _F32)],
        compiler_params=pltpu.CompilerParams(dimension_semantics=("parallel",),
                                             vmem_limit_bytes=VMEM_LIMIT),
        name="in_proj",
    )(xm, w_in_p)


def _band_bias(mq, mk, max_dist, unit, slopes):
    mq = jnp.asarray(mq, jnp.int32)[:, None]
    mk = jnp.asarray(mk, jnp.int32)[None, :]
    dist = mq - mk
    valid = (dist >= 0) & (dist <= max_dist)
    first = valid & (mk >= 0)
    pen = -(jnp.asarray(slopes)[:, None, None] * (dist * unit).astype(_F32)[None])
    neg = jnp.float32(-jnp.inf)
    return jnp.stack([jnp.where(first[None], pen, neg), jnp.where(valid[None], pen, neg)], 0)


def _attn_body(*refs, shared_kv, head_of, has_sink, emit_stats):
    q_ref, kp_ref, kc_ref, vp_ref, vc_ref, bias_ref = refs[:6]
    pos = 6
    sink_ref = None
    if has_sink:
        sink_ref = refs[pos]
        pos += 1
    o_ref = refs[pos]
    st_ref = refs[pos + 1] if emit_stats else None

    qw = q_ref.shape[-1]
    kw = kc_ref.shape[-1]
    q = q_ref[...].reshape(BLOCK, qw)
    k = jnp.concatenate([kp_ref[...].reshape(BLOCK, kw), kc_ref[...].reshape(BLOCK, kw)], axis=0)
    v = jnp.concatenate([vp_ref[...].reshape(BLOCK, kw), vc_ref[...].reshape(BLOCK, kw)], axis=0)
    lane = lax.broadcasted_iota(jnp.int32, (BLOCK, LANES), 1)
    low = lane < HEAD_DIM
    zero = jnp.zeros((BLOCK, LANES), _BF16)
    stats = jnp.zeros((BLOCK, LANES), _F32)
    for j in range(qw // LANES):
        qj = q[:, j * LANES:(j + 1) * LANES]
        if shared_kv:
            kj, vj = k, v
        else:
            kj, vj = k[:, j * LANES:(j + 1) * LANES], v[:, j * LANES:(j + 1) * LANES]
        halves = []
        for half in range(2):
            h = head_of(j, half)
            qm = jnp.where(low, qj, zero) if half == 0 else jnp.where(low, zero, qj)
            s = _dot_nt(qm, kj) + bias_ref[h]
            m = jnp.max(s, axis=-1, keepdims=True)
            if has_sink:
                sk = sink_ref[h]
                m = jnp.maximum(m, sk)
            p = jnp.exp(s - m)
            l = jnp.sum(p, axis=-1, keepdims=True)
            if has_sink:
                l = l + jnp.exp(sk - m)
            pv = _dot(p.astype(_BF16), vj)
            halves.append(pv * (1.0 / l))
            if emit_stats:
                stats = jnp.where(lane == h, m + jnp.log(l), stats)
        oj = jnp.where(low, halves[0], halves[1]).astype(o_ref.dtype)
        o_ref[..., j * LANES:(j + 1) * LANES] = oj.reshape(o_ref.shape[:-1] + (LANES,))
    if emit_stats:
        st_ref[...] = stats.reshape(st_ref.shape)


def _banded_attention(q, k, v, bias, *, grid, blk, q_map, prev_map, shared_kv, head_of,
                      sinks=None, emit_stats=False, name):
    qw, kw = q.shape[-1], k.shape[-1]
    spec = lambda w, imap: pl.BlockSpec(blk + (w,), imap)
    first_sel = lambda *g: (jnp.where(g[-1] == 0, 0, 1), 0, 0, 0)
    in_specs = [spec(qw, q_map), spec(kw, prev_map), spec(kw, q_map), spec(kw, prev_map), spec(kw, q_map),
                pl.BlockSpec((None,) + bias.shape[1:], first_sel)]
    args = [q, k, k, v, v, bias]
    if sinks is not None:
        in_specs.append(pl.BlockSpec(memory_space=pltpu.SMEM))
        args.append(sinks)
    out_shape = [jax.ShapeDtypeStruct(q.shape, _BF16)]
    out_specs = [spec(qw, q_map)]
    if emit_stats:
        out_shape.append(jax.ShapeDtypeStruct(q.shape[:-1] + (LANES,), _F32))
        out_specs.append(spec(LANES, q_map))
    body = functools.partial(_attn_body, shared_kv=shared_kv, head_of=head_of,
                             has_sink=sinks is not None, emit_stats=emit_stats)
    return pl.pallas_call(
        body, grid=grid, in_specs=in_specs, out_specs=out_specs, out_shape=out_shape,
        compiler_params=pltpu.CompilerParams(dimension_semantics=("parallel",) * len(grid),
                                             vmem_limit_bytes=VMEM_LIMIT),
        name=name,
    )(*args)


def _head_a(j, half):
    return j + 4 * half


def _head_b(j, half):
    return 2 * j + half


def _mixer_a(qa, ka, va, sinks, b, s):
    nb = s // BLOCK
    bias = _band_bias(np.arange(BLOCK), np.arange(2 * BLOCK) - BLOCK, A_MAX_DIST, 1, _alibi_slopes(Q_HEADS))
    q3, k3, v3 = (t.reshape(b, s, t.shape[-1]) for t in (qa, ka, va))
    (oa,) = _banded_attention(
        q3, k3, v3, bias, grid=(b, nb), blk=(None, BLOCK),
        q_map=lambda bi, n: (bi, n, 0), prev_map=lambda bi, n: (bi, jnp.maximum(n - 1, 0), 0),
        shared_kv=True, head_of=_head_a, sinks=sinks, name="mixer_a")
    return oa.reshape(b * s, QW)


def _mixer_b(qb, kb, vb, qbp, kbp, vbp, b, s):
    slopes = _alibi_slopes(Q_HEADS)
    nt = s // TM
    outs = []
    w, r = B_PATTERNS[0]
    bias = _band_bias(np.arange(BLOCK), np.arange(2 * BLOCK) - BLOCK, w // r, r, slopes)
    q3, k3, v3 = (t.reshape(b, s, QW) for t in (qb, kb, vb))
    o1, st1 = _banded_attention(
        q3, k3, v3, bias, grid=(b, s // BLOCK), blk=(None, BLOCK),
        q_map=lambda bi, n: (bi, n, 0), prev_map=lambda bi, n: (bi, jnp.maximum(n - 1, 0), 0),
        shared_kv=False, head_of=_head_b, emit_stats=True, name="mixer_b_d1")
    outs.append((o1.reshape(b * s, QW), st1.reshape(b * s, LANES)))
    w, r = B_PATTERNS[1]
    sub = (4 * np.arange(TI)[None, :] + np.arange(4)[:, None]).reshape(-1)
    bias = _band_bias(sub, np.concatenate([sub - BLOCK, sub]), w // r, r, slopes)
    view = lambda t: t.reshape(b, nt, 4, 4, TI, t.shape[-1])
    o4, st4 = _banded_attention(
        view(qbp), view(kbp), view(vbp), bias, grid=(b, 4, nt), blk=(None, None, 4, None, TI),
        q_map=lambda bi, c, n: (bi, n, 0, c, 0, 0),
        prev_map=lambda bi, c, n: (bi, jnp.maximum(n - 1, 0), 0, c, 0, 0),
        shared_kv=False, head_of=_head_b, emit_stats=True, name="mixer_b_d4")
    outs.append((o4.reshape(b * nt, DIL, TI, QW), st4.reshape(b * nt, DIL, TI, LANES)))
    w, r = B_PATTERNS[2]
    bias = _band_bias(np.arange(BLOCK), np.arange(2 * BLOCK) - BLOCK, w // r, r, slopes)
    view = lambda t: t.reshape(b, nt // 4, 4, DIL, TI, t.shape[-1])
    o16, st16 = _banded_attention(
        view(qbp), view(kbp), view(vbp), bias, grid=(b, DIL, nt // 4), blk=(None, None, 4, None, TI),
        q_map=lambda bi, c, n: (bi, n, 0, c, 0, 0),
        prev_map=lambda bi, c, n: (bi, jnp.maximum(n - 1, 0), 0, c, 0, 0),
        shared_kv=False, head_of=_head_b, emit_stats=True, name="mixer_b_d16")
    outs.append((o16.reshape(b * nt, DIL, TI, QW), st16.reshape(b * nt, DIL, TI, LANES)))
    return outs


def _layer_norm(y, g, b):
    mu = jnp.mean(y, axis=-1, keepdims=True)
    yc = y - mu
    var = jnp.mean(yc * yc, axis=-1, keepdims=True)
    return yc * lax.rsqrt(var + LN_EPS) * g + b


def _rms_norm(y, g):
    ms = jnp.mean(y * y, axis=-1, keepdims=True)
    return y * lax.rsqrt(ms + RMS_EPS) * g


def _mix_body(x_ref, oa_ref, o1_ref, st1_ref, o4_ref, st4_ref, o16_ref, st16_ref,
              expand_ref, wo_ref, ga_ref, gb_ref, lg_ref, lb_ref, h_ref, slab_ref):
    n_q = QW // LANES
    for base, o_ref, st_ref in ((0, o4_ref, st4_ref), (n_q + 1, o16_ref, st16_ref)):
        for c in range(DIL):
            for j in range(n_q):
                slab_ref[base + j, pl.ds(c, TI, stride=DIL), :] = (
                    o_ref[c, :, j * LANES:(j + 1) * LANES].astype(_F32))
            slab_ref[base + n_q, pl.ds(c, TI, stride=DIL), :] = st_ref[c]
    lse1, lse4, lse16 = st1_ref[...], slab_ref[n_q], slab_ref[2 * n_q + 1]
    mx = jnp.maximum(jnp.maximum(lse1, lse4), lse16)
    e1, e4, e16 = jnp.exp(lse1 - mx), jnp.exp(lse4 - mx), jnp.exp(lse16 - mx)
    inv = 1.0 / (e1 + e4 + e16)
    spread = lambda wgt: jnp.dot(wgt, expand_ref[...], preferred_element_type=_F32,
                                 precision=lax.Precision.HIGHEST)
    w1, w4, w16 = spread(e1 * inv), spread(e4 * inv), spread(e16 * inv)
    o4 = jnp.concatenate([slab_ref[j] for j in range(n_q)], axis=1)
    o16 = jnp.concatenate([slab_ref[n_q + 1 + j] for j in range(n_q)], axis=1)
    ob = w1 * o1_ref[...].astype(_F32) + w4 * o4 + w16 * o16
    obn = _rms_norm(ob, gb_ref[...]).astype(_BF16)
    oan = _rms_norm(oa_ref[...].astype(_F32), ga_ref[...]).astype(_BF16)
    mix = _dot(oan, wo_ref[0:QW, :]) + _dot(obn, wo_ref[QW:2 * QW, :])
    h_ref[...] = _layer_norm(ALPHA * x_ref[...] + mix, lg_ref[...], lb_ref[...])


def _mix_out(xm, oa, b_outs, expand, wo_p, ga_p, gb, lg, lb):
    m, d = xm.shape
    nt = m // TM
    (o1, st1), (o4, st4), (o16, st16) = b_outs
    row = lambda w: pl.BlockSpec((TM, w), lambda i: (i, 0))
    perm = lambda w: pl.BlockSpec((None, DIL, TI, w), lambda i: (i, 0, 0, 0))
    const = lambda a: pl.BlockSpec(a.shape, lambda i: (0,) * a.ndim, pipeline_mode=pl.Buffered(1))
    return pl.pallas_call(
        _mix_body,
        grid=(nt,),
        in_specs=[row(d), row(QW), row(QW), row(LANES), perm(QW), perm(LANES), perm(QW), perm(LANES),
                  const(expand), const(wo_p), const(ga_p), const(gb), const(lg), const(lb)],
        out_specs=row(d),
        out_shape=jax.ShapeDtypeStruct((m, d), _F32),
        scratch_shapes=[pltpu.VMEM((2 * (QW // LANES + 1), TM, LANES), _F32)],
        compiler_params=pltpu.CompilerParams(dimension_semantics=("parallel",),
                                             vmem_limit_bytes=VMEM_LIMIT),
        name="mix_out",
    )(xm, oa, o1, st1, o4, st4, o16, st16, expand, wo_p, ga_p, gb, lg, lb)


def _gelu_tanh(x):
    c = math.sqrt(2.0 / math.pi)
    return x * (0.5 * (1.0 + jnp.tanh(c * (x + 0.044715 * (x * x * x)))))


def _ffn_body(h_ref, halo_ref, wg_ref, wv_ref, cg_ref, cv_ref, wd_ref, lg_ref, lb_ref, o_ref,
              acc_ref, *, tiles_per_seq):
    i = pl.program_id(0)
    h = h_ref[...]
    keep = (i % tiles_per_seq != 0).astype(_F32)
    hx = jnp.concatenate([halo_ref[...] * keep, h], axis=0).astype(_BF16)
    acc_ref[...] = jnp.zeros_like(acc_ref)

    def conv(u, cw):
        y = cw[2:3] * u + cw[1:2] * pltpu.roll(u, 1, 0) + cw[0:1] * pltpu.roll(u, 2, 0) + cw[3:4]
        return y[HALO:]

    def chunk(j, carry):
        gate = conv(_dot(hx, wg_ref[j]), cg_ref[j])
        val = conv(_dot(hx, wv_ref[j]), cv_ref[j])
        act = (_gelu_tanh(gate) * val).astype(_BF16)
        acc_ref[...] += _dot(act, wd_ref[j])
        return carry

    lax.fori_loop(0, N_FF_CHUNKS, chunk, 0)
    o_ref[...] = _layer_norm(ALPHA * h + acc_ref[...], lg_ref[...], lb_ref[...])


def _ffn(h1, wg, wv, cg, cv, wd, lg, lb, s):
    m, d = h1.shape
    nt = m // TM
    halo_blocks = TM // HALO
    const = lambda a: pl.BlockSpec(a.shape, lambda i: (0,) * a.ndim, pipeline_mode=pl.Buffered(1))
    return pl.pallas_call(
        functools.partial(_ffn_body, tiles_per_seq=s // TM),
        grid=(nt,),
        in_specs=[pl.BlockSpec((TM, d), lambda i: (i, 0)),
                  pl.BlockSpec((HALO, d), lambda i: (jnp.maximum(i * halo_blocks - 1, 0), 0)),
                  const(wg), const(wv), const(cg), const(cv), const(wd), const(lg), const(lb)],
        out_specs=pl.BlockSpec((TM, d), lambda i: (i, 0)),
        out_shape=jax.ShapeDtypeStruct((m, d), _F32),
        scratch_shapes=[pltpu.VMEM((TM, d), _F32)],
        compiler_params=pltpu.CompilerParams(dimension_semantics=("parallel",),
                                             vmem_limit_bytes=VMEM_LIMIT),
        name="conv_ffn",
    )(h1, h1, wg, wv, cg, cv, wd, lg, lb)


def _conv_table(conv_w, conv_b, lo):
    w = conv_w[:, lo:lo + D_FF].reshape(3, N_FF_CHUNKS, FF_CHUNK)
    bias = conv_b[lo:lo + D_FF].reshape(1, N_FF_CHUNKS, FF_CHUNK)
    t = jnp.concatenate([w, bias, jnp.zeros((4, N_FF_CHUNKS, FF_CHUNK), _F32)], axis=0)
    return t.transpose(1, 0, 2)


def kernel(x, w_in, norm_a_g, norm_b_g, sinks_a, w_o, ln1_g, ln1_b, w_up, conv_w, conv_b, w_down, ln2_g, ln2_b):
    b, s, d = x.shape
    assert s % (4 * TM) == 0 and d == 1024 and w_up.shape[1] == 2 * D_FF
    head_order = np.array([0, 4, 1, 5, 2, 6, 3, 7])
    a_cols = (head_order[:, None] * HEAD_DIM + np.arange(HEAD_DIM)[None, :]).reshape(-1)
    scale = 1.0 / math.sqrt(HEAD_DIM)
    off_b = QW + 2 * A_KV_W
    w_in_p = jnp.concatenate([w_in[:, a_cols] * scale, w_in[:, QW:off_b],
                              w_in[:, off_b:off_b + QW] * scale, w_in[:, off_b + QW:]], axis=1).astype(_BF16)
    wo_p = jnp.concatenate([w_o[a_cols], w_o[QW:]], axis=0).astype(_BF16)
    ga_p = norm_a_g[a_cols].reshape(1, QW)
    gb = norm_b_g.reshape(1, QW)
    expand = jnp.asarray((np.arange(LANES)[:, None] == np.arange(QW)[None, :] // HEAD_DIM), _F32)
    chunked = lambda w: w.reshape(d, N_FF_CHUNKS, FF_CHUNK).transpose(1, 0, 2).astype(_BF16)
    wg, wv = chunked(w_up[:, :D_FF]), chunked(w_up[:, D_FF:])
    wd = w_down.reshape(N_FF_CHUNKS, FF_CHUNK, d).astype(_BF16)
    cg, cv = _conv_table(conv_w, conv_b, 0), _conv_table(conv_w, conv_b, D_FF)

    xm = x.reshape(b * s, d)
    qa, ka, va, qb, kb, vb, qbp, kbp, vbp = _project(xm, w_in_p)
    oa = _mixer_a(qa, ka, va, sinks_a, b, s)
    b_outs = _mixer_b(qb, kb, vb, qbp, kbp, vbp, b, s)
    h1 = _mix_out(xm, oa, b_outs, expand, wo_p, ga_p, gb, ln1_g.reshape(1, d), ln1_b.reshape(1, d))
    out = _ffn(h1, wg, wv, cg, cv, wd, ln2_g.reshape(1, d), ln2_b.reshape(1, d), s)
    return out.reshape(b, s, d)
```

```python
import functools
import math

import numpy as np
import jax
import jax.numpy as jnp
from jax import lax
from jax.experimental import pallas as pl
from jax.experimental.pallas import tpu as pltpu

HEAD_DIM = 64
LANES = 128
Q_HEADS = 8
QW = Q_HEADS * HEAD_DIM
A_KV_W = 128
BLOCK = 128
A_MAX_DIST = 127
B_PATTERNS = ((128, 1), (512, 4), (2048, 16))
DIL = 16
TI = 32
TM = DIL * TI
D_FF = 2816
FF_CHUNK = 256
N_FF_CHUNKS = D_FF // FF_CHUNK
HALO = 16
ALPHA = 2.0 ** 0.25
LN_EPS = 1e-5
RMS_EPS = 1e-6
VMEM_LIMIT = 48 * 1024 * 1024
NAT_BLOCKS = 8
D4_BLOCKS = 4
D16_RESIDUES = 2

_BF16 = jnp.bfloat16
_F32 = jnp.float32


def _alibi_slopes(n):
    return np.array([2.0 ** (-8.0 * (i + 1) / n) for i in range(n)], dtype=np.float32)


def _dot(a, b):
    return jnp.dot(a, b, preferred_element_type=_F32)


def _dot_nt(a, b):
    return lax.dot_general(a, b, (((1,), (1,)), ((), ())), preferred_element_type=_F32)


def _proj_body(x_ref, w_ref, qa_ref, ka_ref, va_ref, qb_ref, kb_ref, vb_ref,
               qbp_ref, kbp_ref, vbp_ref, slab_ref):
    xb = x_ref[...].astype(_BF16)
    ya = _dot(xb, w_ref[:, 0:QW + 2 * A_KV_W])
    qa_ref[...] = ya[:, 0:QW].astype(_BF16)
    ka_ref[...] = ya[:, QW:QW + A_KV_W].astype(_BF16)
    va_ref[...] = ya[:, QW + A_KV_W:QW + 2 * A_KV_W].astype(_BF16)
    base = QW + 2 * A_KV_W
    for idx, (nat_ref, perm_ref) in enumerate(((qb_ref, qbp_ref), (kb_ref, kbp_ref), (vb_ref, vbp_ref))):
        y = _dot(xb, w_ref[:, base + QW * idx:base + QW * (idx + 1)])
        nat_ref[...] = y.astype(_BF16)
        for j in range(QW // LANES):
            slab_ref[j] = y[:, j * LANES:(j + 1) * LANES]
        for j in range(QW // LANES):
            for c in range(DIL):
                perm_ref[c, :, j * LANES:(j + 1) * LANES] = (
                    slab_ref[j, pl.ds(c, TI, stride=DIL), :].astype(_BF16))


def _project(xm, w_in_p):
    m, d = xm.shape
    nt = m // TM
    nat = lambda w: jax.ShapeDtypeStruct((m, w), _BF16)
    perm = jax.ShapeDtypeStruct((nt, DIL, TI, QW), _BF16)
    nat_spec = lambda w: pl.BlockSpec((TM, w), lambda i: (i, 0))
    perm_spec = pl.BlockSpec((None, DIL, TI, QW), lambda i: (i, 0, 0, 0))
    return pl.pallas_call(
        _proj_body,
        grid=(nt,),
        in_specs=[pl.BlockSpec((TM, d), lambda i: (i, 0)),
                  pl.BlockSpec(w_in_p.shape, lambda i: (0, 0), pipeline_mode=pl.Buffered(1))],
        out_specs=[nat_spec(QW), nat_spec(A_KV_W), nat_spec(A_KV_W),
                   nat_spec(QW), nat_spec(QW), nat_spec(QW), perm_spec, perm_spec, perm_spec],
        out_shape=[nat(QW), nat(A_KV_W), nat(A_KV_W), nat(QW), nat(QW), nat(QW), perm, perm, perm],
        scratch_shapes=[pltpu.VMEM((QW // LANES, TM, LANES), _F32)],
        compiler_params=pltpu.CompilerParams(dimension_semantics=("parallel",),
                                             vmem_limit_bytes=VMEM_LIMIT),
        name="in_proj",
    )(xm, w_in_p)


def _band_bias(mq, mk, max_dist, unit, slopes):
    mq = jnp.asarray(mq, jnp.int32)[:, None]
    mk = jnp.asarray(mk, jnp.int32)[None, :]
    dist = mq - mk
    valid = (dist >= 0) & (dist <= max_dist)
    first = valid & (mk >= 0)
    pen = -(jnp.asarray(slopes)[:, None, None] * (dist * unit).astype(_F32)[None])
    neg = jnp.float32(-jnp.inf)
    return jnp.stack([jnp.where(first[None], pen, neg), jnp.where(valid[None], pen, neg)], 0)


def _attn_body(*refs, units, has_prev, seq_axis, shared_kv, head_of, has_sink, emit_stats):
    refs = list(refs)
    q_ref = refs.pop(0)
    kp_ref = refs.pop(0) if has_prev else None
    vp_ref = refs.pop(0) if has_prev else None
    kc_ref, vc_ref, bias_ref = refs.pop(0), refs.pop(0), refs.pop(0)
    sink_ref = refs.pop(0) if has_sink else None
    o_ref = refs.pop(0)
    st_ref = refs.pop(0) if emit_stats else None

    qw, kw = q_ref.shape[-1], kc_ref.shape[-1]
    lane = lax.broadcasted_iota(jnp.int32, (BLOCK, LANES), 1)
    low = lane < HEAD_DIM
    zero = jnp.zeros((BLOCK, LANES), _BF16)
    blk_shape = q_ref[units[0][0]].shape[:-1]

    for cur, prev in units:
        q = q_ref[cur].reshape(BLOCK, qw)
        k_cur, v_cur = kc_ref[cur].reshape(BLOCK, kw), vc_ref[cur].reshape(BLOCK, kw)
        if prev is not None:
            k_prev, v_prev = kc_ref[prev].reshape(BLOCK, kw), vc_ref[prev].reshape(BLOCK, kw)
            sel = 1
        elif has_prev:
            k_prev, v_prev = kp_ref[0].reshape(BLOCK, kw), vp_ref[0].reshape(BLOCK, kw)
            sel = jnp.where(pl.program_id(seq_axis) == 0, 0, 1)
        else:
            k_prev, v_prev = k_cur, v_cur
            sel = 0
        k = jnp.concatenate([k_prev, k_cur], axis=0)
        v = jnp.concatenate([v_prev, v_cur], axis=0)
        stats = jnp.zeros((BLOCK, LANES), _F32)
        for j in range(qw // LANES):
            qj = q[:, j * LANES:(j + 1) * LANES]
            if shared_kv:
                kj, vj = k, v
            else:
                kj, vj = k[:, j * LANES:(j + 1) * LANES], v[:, j * LANES:(j + 1) * LANES]
            halves = []
            for half in range(2):
                h = head_of(j, half)
                qm = jnp.where(low, qj, zero) if half == 0 else jnp.where(low, zero, qj)
                s = _dot_nt(qm, kj) + bias_ref[sel, h]
                m = jnp.max(s, axis=-1, keepdims=True)
                if has_sink:
                    sk = sink_ref[h]
                    m = jnp.maximum(m, sk)
                p = jnp.exp(s - m)
                l = jnp.sum(p, axis=-1, keepdims=True)
                if has_sink:
                    l = l + jnp.exp(sk - m)
                pv = _dot(p.astype(_BF16), vj)
                halves.append(pv * (1.0 / l))
                if emit_stats:
                    stats = jnp.where(lane == h, m + jnp.log(l), stats)
            oj = jnp.where(low, halves[0], halves[1]).astype(o_ref.dtype)
            o_ref[cur + (Ellipsis, slice(j * LANES, (j + 1) * LANES))] = oj.reshape(blk_shape + (LANES,))
        if emit_stats:
            st_ref[cur] = stats.reshape(blk_shape + (LANES,))


def _banded_attention(q, k, v, bias, *, grid, blk, q_map, prev_blk, prev_map, units, shared_kv,
                      head_of, sinks=None, emit_stats=False, name):
    qw, kw = q.shape[-1], k.shape[-1]
    spec = lambda shape, w, imap: pl.BlockSpec(shape + (w,), imap)
    has_prev = prev_blk is not None
    in_specs, args = [spec(blk, qw, q_map)], [q]
    if has_prev:
        in_specs += [spec(prev_blk, kw, prev_map), spec(prev_blk, kw, prev_map)]
        args += [k, v]
    in_specs += [spec(blk, kw, q_map), spec(blk, kw, q_map),
                 pl.BlockSpec(bias.shape, lambda *g: (0, 0, 0, 0), pipeline_mode=pl.Buffered(1))]
    args += [k, v, bias]
    if sinks is not None:
        in_specs.append(pl.BlockSpec(memory_space=pltpu.SMEM))
        args.append(sinks)
    out_shape = [jax.ShapeDtypeStruct(q.shape, _BF16)]
    out_specs = [spec(blk, qw, q_map)]
    if emit_stats:
        out_shape.append(jax.ShapeDtypeStruct(q.shape[:-1] + (LANES,), _F32))
        out_specs.append(spec(blk, LANES, q_map))
    body = functools.partial(_attn_body, units=units, has_prev=has_prev, seq_axis=len(grid) - 1,
                             shared_kv=shared_kv, head_of=head_of, has_sink=sinks is not None,
                             emit_stats=emit_stats)
    return pl.pallas_call(
        body, grid=grid, in_specs=in_specs, out_specs=out_specs, out_shape=out_shape,
        compiler_params=pltpu.CompilerParams(dimension_semantics=("parallel",) * len(grid),
                                             vmem_limit_bytes=VMEM_LIMIT),
        name=name,
    )(*args)


def _head_a(j, half):
    return j + 4 * half


def _head_b(j, half):
    return 2 * j + half


def _chain(n):
    return [((i,), (i - 1,) if i else None) for i in range(n)]


def _token_order_attention(q, k, v, bias, b, s, **kw):
    nb = s // BLOCK
    view = lambda t: t.reshape(b, nb, BLOCK, t.shape[-1])
    outs = _banded_attention(
        view(q), view(k), view(v), bias, grid=(b, nb // NAT_BLOCKS),
        blk=(None, NAT_BLOCKS, BLOCK), q_map=lambda bi, n: (bi, n, 0, 0),
        prev_blk=(None, 1, BLOCK), prev_map=lambda bi, n: (bi, jnp.maximum(n * NAT_BLOCKS - 1, 0), 0, 0),
        units=_chain(NAT_BLOCKS), **kw)
    return [o.reshape(b * s, o.shape[-1]) for o in outs]


def _mixer_a(qa, ka, va, sinks, b, s):
    bias = _band_bias(np.arange(BLOCK), np.arange(2 * BLOCK) - BLOCK, A_MAX_DIST, 1, _alibi_slopes(Q_HEADS))
    (oa,) = _token_order_attention(qa, ka, va, bias, b, s, shared_kv=True, head_of=_head_a,
                                   sinks=sinks, name="mixer_a")
    return oa


def _mixer_b(qb, kb, vb, qbp, kbp, vbp, b, s):
    slopes = _alibi_slopes(Q_HEADS)
    nt = s // TM
    common = dict(shared_kv=False, head_of=_head_b, emit_stats=True)
    outs = []
    w, r = B_PATTERNS[0]
    bias = _band_bias(np.arange(BLOCK), np.arange(2 * BLOCK) - BLOCK, w // r, r, slopes)
    outs.append(tuple(_token_order_attention(qb, kb, vb, bias, b, s, name="mixer_b_d1", **common)))
    w, r = B_PATTERNS[1]
    sub = (4 * np.arange(TI)[None, :] + np.arange(4)[:, None]).reshape(-1)
    bias = _band_bias(sub, np.concatenate([sub - BLOCK, sub]), w // r, r, slopes)
    view = lambda t: t.reshape(b, nt, 4, 4, TI, t.shape[-1])
    o4, st4 = _banded_attention(
        view(qbp), view(kbp), view(vbp), bias, grid=(b, 4, nt // D4_BLOCKS),
        blk=(None, D4_BLOCKS, 4, None, TI), q_map=lambda bi, c, n: (bi, n, 0, c, 0, 0),
        prev_blk=(None, 1, 4, None, TI),
        prev_map=lambda bi, c, n: (bi, jnp.maximum(n * D4_BLOCKS - 1, 0), 0, c, 0, 0),
        units=_chain(D4_BLOCKS), name="mixer_b_d4", **common)
    outs.append((o4.reshape(b * nt, DIL, TI, QW), st4.reshape(b * nt, DIL, TI, LANES)))
    w, r = B_PATTERNS[2]
    bias = _band_bias(np.arange(BLOCK), np.arange(2 * BLOCK) - BLOCK, w // r, r, slopes)
    ng = nt // 4
    view = lambda t: t.reshape(b, ng, 4, DIL, TI, t.shape[-1])
    units = [((g, slice(None), c), (g - 1, slice(None), c) if g else None)
             for c in range(D16_RESIDUES) for g in range(ng)]
    o16, st16 = _banded_attention(
        view(qbp), view(kbp), view(vbp), bias, grid=(b, DIL // D16_RESIDUES),
        blk=(None, ng, 4, D16_RESIDUES, TI), q_map=lambda bi, c: (bi, 0, 0, c, 0, 0),
        prev_blk=None, prev_map=None, units=units, name="mixer_b_d16", **common)
    outs.append((o16.reshape(b * nt, DIL, TI, QW), st16.reshape(b * nt, DIL, TI, LANES)))
    return outs


def _layer_norm(y, g, b):
    mu = jnp.mean(y, axis=-1, keepdims=True)
    yc = y - mu
    var = jnp.mean(yc * yc, axis=-1, keepdims=True)
    return yc * lax.rsqrt(var + LN_EPS) * g + b


def _rms_norm(y, g):
    ms = jnp.mean(y * y, axis=-1, keepdims=True)
    return y * lax.rsqrt(ms + RMS_EPS) * g


def _mix_body(x_ref, oa_ref, o1_ref, st1_ref, o4_ref, st4_ref, o16_ref, st16_ref,
              wo_ref, ga_ref, gb_ref, lg_ref, lb_ref, h_ref, slab_ref):
    n_q = QW // LANES
    for base, o_ref, st_ref in ((0, o4_ref, st4_ref), (n_q + 1, o16_ref, st16_ref)):
        for c in range(DIL):
            for j in range(n_q):
                slab_ref[base + j, pl.ds(c, TI, stride=DIL), :] = (
                    o_ref[c, :, j * LANES:(j + 1) * LANES].astype(_F32))
            slab_ref[base + n_q, pl.ds(c, TI, stride=DIL), :] = st_ref[c]
    lse1, lse4, lse16 = st1_ref[...], slab_ref[n_q], slab_ref[2 * n_q + 1]
    mx = jnp.maximum(jnp.maximum(lse1, lse4), lse16)
    e1, e4, e16 = jnp.exp(lse1 - mx), jnp.exp(lse4 - mx), jnp.exp(lse16 - mx)
    inv = 1.0 / (e1 + e4 + e16)
    w1, w4, w16 = e1 * inv, e4 * inv, e16 * inv
    low = lax.broadcasted_iota(jnp.int32, (TM, LANES), 1) < HEAD_DIM
    parts = []
    for j in range(n_q):
        spread = lambda wgt: jnp.where(low, wgt[:, 2 * j:2 * j + 1], wgt[:, 2 * j + 1:2 * j + 2])
        lanes = slice(j * LANES, (j + 1) * LANES)
        parts.append(spread(w1) * o1_ref[:, lanes].astype(_F32) + spread(w4) * slab_ref[j]
                     + spread(w16) * slab_ref[n_q + 1 + j])
    ob = jnp.concatenate(parts, axis=1)
    obn = _rms_norm(ob, gb_ref[...]).astype(_BF16)
    oan = _rms_norm(oa_ref[...].astype(_F32), ga_ref[...]).astype(_BF16)
    mix = _dot(oan, wo_ref[0:QW, :]) + _dot(obn, wo_ref[QW:2 * QW, :])
    h_ref[...] = _layer_norm(ALPHA * x_ref[...] + mix, lg_ref[...], lb_ref[...])


def _mix_out(xm, oa, b_outs, wo_p, ga_p, gb, lg, lb):
    m, d = xm.shape
    nt = m // TM
    (o1, st1), (o4, st4), (o16, st16) = b_outs
    row = lambda w: pl.BlockSpec((TM, w), lambda i: (i, 0))
    perm = lambda w: pl.BlockSpec((None, DIL, TI, w), lambda i: (i, 0, 0, 0))
    const = lambda a: pl.BlockSpec(a.shape, lambda i: (0,) * a.ndim, pipeline_mode=pl.Buffered(1))
    return pl.pallas_call(
        _mix_body,
        grid=(nt,),
        in_specs=[row(d), row(QW), row(QW), row(LANES), perm(QW), perm(LANES), perm(QW), perm(LANES),
                  const(wo_p), const(ga_p), const(gb), const(lg), const(lb)],
        out_specs=row(d),
        out_shape=jax.ShapeDtypeStruct((m, d), _F32),
        scratch_shapes=[pltpu.VMEM((2 * (QW // LANES + 1), TM, LANES), _F32)],
        compiler_params=pltpu.CompilerParams(dimension_semantics=("parallel",),
                                             vmem_limit_bytes=VMEM_LIMIT),
        name="mix_out",
    )(xm, oa, o1, st1, o4, st4, o16, st16, wo_p, ga_p, gb, lg, lb)


def _gelu_tanh(x):
    c = math.sqrt(2.0 / math.pi)
    return x * (0.5 * (1.0 + jnp.tanh(c * (x + 0.044715 * (x * x * x)))))


def _ffn_body(h_ref, halo_ref, wg_ref, wv_ref, cg_ref, cv_ref, wd_ref, lg_ref, lb_ref, o_ref,
              act_ref, *, tiles_per_seq):
    i = pl.program_id(0)
    h = h_ref[...]
    keep = (i % tiles_per_seq != 0).astype(_F32)
    hx = jnp.concatenate([halo_ref[...] * keep, h], axis=0).astype(_BF16)

    def conv(u, cw):
        y = cw[2:3] * u + cw[1:2] * pltpu.roll(u, 1, 0) + cw[0:1] * pltpu.roll(u, 2, 0) + cw[3:4]
        return y[HALO:]

    for j in range(N_FF_CHUNKS):
        gate = conv(_dot(hx, wg_ref[j]), cg_ref[j])
        val = conv(_dot(hx, wv_ref[j]), cv_ref[j])
        act_ref[:, j * FF_CHUNK:(j + 1) * FF_CHUNK] = (_gelu_tanh(gate) * val).astype(_BF16)
    ff = _dot(act_ref[...], wd_ref[...])
    o_ref[...] = _layer_norm(ALPHA * h + ff, lg_ref[...], lb_ref[...])


def _ffn(h1, wg, wv, cg, cv, wd, lg, lb, s):
    m, d = h1.shape
    nt = m // TM
    halo_blocks = TM // HALO
    const = lambda a: pl.BlockSpec(a.shape, lambda i: (0,) * a.ndim, pipeline_mode=pl.Buffered(1))
    return pl.pallas_call(
        functools.partial(_ffn_body, tiles_per_seq=s // TM),
        grid=(nt,),
        in_specs=[pl.BlockSpec((TM, d), lambda i: (i, 0)),
                  pl.BlockSpec((HALO, d), lambda i: (jnp.maximum(i * halo_blocks - 1, 0), 0)),
                  const(wg), const(wv), const(cg), const(cv), const(wd), const(lg), const(lb)],
        out_specs=pl.BlockSpec((TM, d), lambda i: (i, 0)),
        out_shape=jax.ShapeDtypeStruct((m, d), _F32),
        scratch_shapes=[pltpu.VMEM((TM, D_FF), _BF16)],
        compiler_params=pltpu.CompilerParams(dimension_semantics=("parallel",),
                                             vmem_limit_bytes=VMEM_LIMIT),
        name="conv_ffn",
    )(h1, h1, wg, wv, cg, cv, wd, lg, lb)


def _conv_table(conv_w, conv_b, lo):
    w = conv_w[:, lo:lo + D_FF].reshape(3, N_FF_CHUNKS, FF_CHUNK)
    bias = conv_b[lo:lo + D_FF].reshape(1, N_FF_CHUNKS, FF_CHUNK)
    t = jnp.concatenate([w, bias, jnp.zeros((4, N_FF_CHUNKS, FF_CHUNK), _F32)], axis=0)
    return t.transpose(1, 0, 2)


def kernel(x, w_in, norm_a_g, norm_b_g, sinks_a, w_o, ln1_g, ln1_b, w_up, conv_w, conv_b, w_down, ln2_g, ln2_b):
    b, s, d = x.shape
    assert s % (4 * TM) == 0 and (s // BLOCK) % NAT_BLOCKS == 0 and (s // TM) % D4_BLOCKS == 0
    assert d == 1024 and w_up.shape[1] == 2 * D_FF
    head_order = np.array([0, 4, 1, 5, 2, 6, 3, 7])
    a_cols = (head_order[:, None] * HEAD_DIM + np.arange(HEAD_DIM)[None, :]).reshape(-1)
    scale = 1.0 / math.sqrt(HEAD_DIM)
    off_b = QW + 2 * A_KV_W
    w_in_p = jnp.concatenate([w_in[:, a_cols] * scale, w_in[:, QW:off_b],
                              w_in[:, off_b:off_b + QW] * scale, w_in[:, off_b + QW:]], axis=1).astype(_BF16)
    wo_p = jnp.concatenate([w_o[a_cols], w_o[QW:]], axis=0).astype(_BF16)
    ga_p = norm_a_g[a_cols].reshape(1, QW)
    gb = norm_b_g.reshape(1, QW)
    chunked = lambda w: w.reshape(d, N_FF_CHUNKS, FF_CHUNK).transpose(1, 0, 2).astype(_BF16)
    wg, wv = chunked(w_up[:, :D_FF]), chunked(w_up[:, D_FF:])
    wd = w_down.astype(_BF16)
    cg, cv = _conv_table(conv_w, conv_b, 0), _conv_table(conv_w, conv_b, D_FF)

    xm = x.reshape(b * s, d)
    qa, ka, va, qb, kb, vb, qbp, kbp, vbp = _project(xm, w_in_p)
    oa = _mixer_a(qa, ka, va, sinks_a, b, s)
    b_outs = _mixer_b(qb, kb, vb, qbp, kbp, vbp, b, s)
    h1 = _mix_out(xm, oa, b_outs, wo_p, ga_p, gb, ln1_g.reshape(1, d), ln1_b.reshape(1, d))
    out = _ffn(h1, wg, wv, cg, cv, wd, ln2_g.reshape(1, d), ln2_b.reshape(1, d), s)
    return out.reshape(b, s, d)
```

```python
import functools
import math

import numpy as np
import jax
import jax.numpy as jnp
from jax import lax
from jax.experimental import pallas as pl
from jax.experimental.pallas import tpu as pltpu

HEAD_DIM = 64
LANES = 128
SUBLANES = 8
Q_HEADS = 8
QW = Q_HEADS * HEAD_DIM
A_KV_W = 128
BLOCK = 128
A_MAX_DIST = 127
B_PATTERNS = ((128, 1), (512, 4), (2048, 16))
DIL = 16
TI = 32
TM = DIL * TI
GROUPS = TM // BLOCK
D_FF = 2816
FF_CHUNK = 256
N_FF_CHUNKS = D_FF // FF_CHUNK
MIX_PIECES = 4
MIX_EVERY = 2
ALPHA = 2.0 ** 0.25
LN_EPS = 1e-5
RMS_EPS = 1e-6
VMEM_LIMIT = 48 * 1024 * 1024
MIX_FFN_VMEM_LIMIT = 58 * 1024 * 1024
NAT_BLOCKS = 8
D4_BLOCKS = 4
D16_RESIDUES = 2

_BF16 = jnp.bfloat16
_F32 = jnp.float32


def _alibi_slopes(n):
    return np.array([2.0 ** (-8.0 * (i + 1) / n) for i in range(n)], dtype=np.float32)


def _dot(a, b):
    return jnp.dot(a, b, preferred_element_type=_F32)


def _dot_nt(a, b):
    return lax.dot_general(a, b, (((1,), (1,)), ((), ())), preferred_element_type=_F32)


def _group_order(y):
    pieces = [y[c * TI + SUBLANES * u:c * TI + SUBLANES * (u + 1)]
              for u in range(GROUPS) for c in range(DIL)]
    return jnp.concatenate(pieces, axis=0)


def _proj_body(*refs, n_x):
    x_refs, w_ref = refs[:n_x], refs[n_x]
    xp_ref, qa_ref, ka_ref, va_ref, qb_ref, kb_ref, vb_ref, qbp_ref, kbp_ref, vbp_ref = refs[n_x + 1:]
    cols = [jnp.concatenate([x_ref[pl.ds(c, TI, stride=DIL), :] for c in range(DIL)], axis=0)
            for x_ref in x_refs]
    xp = jnp.concatenate(cols, axis=1)
    xp_ref[...] = xp
    xb = xp.astype(_BF16)
    ya = _dot(xb, w_ref[:, 0:QW + 2 * A_KV_W])
    qa_ref[...] = _group_order(ya[:, 0:QW]).astype(_BF16)
    ka_ref[...] = _group_order(ya[:, QW:QW + A_KV_W]).astype(_BF16)
    va_ref[...] = _group_order(ya[:, QW + A_KV_W:QW + 2 * A_KV_W]).astype(_BF16)
    base = QW + 2 * A_KV_W
    for idx, (grp_ref, til_ref) in enumerate(((qb_ref, qbp_ref), (kb_ref, kbp_ref), (vb_ref, vbp_ref))):
        y = _dot(xb, w_ref[:, base + QW * idx:base + QW * (idx + 1)])
        til_ref[...] = y.astype(_BF16)
        grp_ref[...] = _group_order(y).astype(_BF16)


def _project(xm, w_in_p):
    m, d = xm.shape
    nt = m // TM
    n_x = d // LANES
    rows = lambda w, dt=_BF16: jax.ShapeDtypeStruct((m, w), dt)
    spec = lambda w: pl.BlockSpec((TM, w), lambda i: (i, 0))
    widths = (QW, A_KV_W, A_KV_W, QW, QW, QW, QW, QW, QW)
    return pl.pallas_call(
        functools.partial(_proj_body, n_x=n_x),
        grid=(nt,),
        in_specs=[pl.BlockSpec((TM, LANES), lambda i, j=j: (i, j)) for j in range(n_x)]
                 + [pl.BlockSpec(w_in_p.shape, lambda i: (0, 0), pipeline_mode=pl.Buffered(1))],
        out_specs=[spec(d)] + [spec(w) for w in widths],
        out_shape=[rows(d, _F32)] + [rows(w) for w in widths],
        compiler_params=pltpu.CompilerParams(dimension_semantics=("parallel",),
                                             vmem_limit_bytes=VMEM_LIMIT),
        name="in_proj",
    )(*([xm] * n_x), w_in_p)


def _band_bias(mq, mk, max_dist, unit, slopes):
    mq = jnp.asarray(mq, jnp.int32)[:, None]
    mk = jnp.asarray(mk, jnp.int32)[None, :]
    dist = mq - mk
    valid = (dist >= 0) & (dist <= max_dist)
    first = valid & (mk >= 0)
    pen = -(jnp.asarray(slopes)[:, None, None] * (dist * unit).astype(_F32)[None])
    neg = jnp.float32(-jnp.inf)
    return jnp.stack([jnp.where(first[None], pen, neg), jnp.where(valid[None], pen, neg)], 0)


def _attn_body(*refs, units, has_prev, seq_axis, shared_kv, head_of, has_sink, emit_stats, regroup):
    refs = list(refs)
    q_ref = refs.pop(0)
    kp_ref = refs.pop(0) if has_prev else None
    vp_ref = refs.pop(0) if has_prev else None
    kc_ref, vc_ref, bias_ref = refs.pop(0), refs.pop(0), refs.pop(0)
    sink_ref = refs.pop(0) if has_sink else None
    o_ref = refs.pop(0)
    st_ref = refs.pop(0) if emit_stats else None

    qw, kw = q_ref.shape[-1], kc_ref.shape[-1]
    low = lax.broadcasted_iota(jnp.int32, (BLOCK, LANES), 1) < HEAD_DIM
    zero = jnp.zeros((BLOCK, LANES), _BF16)
    blk_shape = q_ref[units[0][0]].shape[:-1]
    held = None

    for n, (cur, prev) in enumerate(units):
        q = q_ref[cur].reshape(BLOCK, qw)
        k_cur, v_cur = kc_ref[cur].reshape(BLOCK, kw), vc_ref[cur].reshape(BLOCK, kw)
        if prev is not None:
            k_prev, v_prev = kc_ref[prev].reshape(BLOCK, kw), vc_ref[prev].reshape(BLOCK, kw)
            sel = 1
        elif has_prev:
            k_prev, v_prev = kp_ref[0].reshape(BLOCK, kw), vp_ref[0].reshape(BLOCK, kw)
            sel = jnp.where(pl.program_id(seq_axis) == 0, 0, 1)
        else:
            k_prev, v_prev = k_cur, v_cur
            sel = 0
        k = jnp.concatenate([k_prev, k_cur], axis=0)
        v = jnp.concatenate([v_prev, v_cur], axis=0)
        outs, stats = [], []
        for j in range(qw // LANES):
            qj = q[:, j * LANES:(j + 1) * LANES]
            if shared_kv:
                kj, vj = k, v
            else:
                kj, vj = k[:, j * LANES:(j + 1) * LANES], v[:, j * LANES:(j + 1) * LANES]
            halves, lses = [], []
            for half in range(2):
                h = head_of(j, half)
                qm = jnp.where(low, qj, zero) if half == 0 else jnp.where(low, zero, qj)
                s = _dot_nt(qm, kj) + bias_ref[sel, h]
                m = jnp.max(s, axis=-1, keepdims=True)
                if has_sink:
                    sk = sink_ref[h]
                    m = jnp.maximum(m, sk)
                p = jnp.exp(s - m)
                l = jnp.sum(p, axis=-1, keepdims=True)
                if has_sink:
                    l = l + jnp.exp(sk - m)
                pv = _dot(p.astype(_BF16), vj)
                halves.append(pv * (1.0 / l))
                lses.append(m + jnp.log(l))
            outs.append(jnp.where(low, halves[0], halves[1]))
            if emit_stats:
                stats.append(jnp.where(low, lses[0], lses[1]))
        results = [(o_ref, outs)] + ([(st_ref, stats)] if emit_stats else [])
        if not regroup:
            for ref, vals in results:
                for j, val in enumerate(vals):
                    ref[cur + (Ellipsis, slice(j * LANES, (j + 1) * LANES))] = (
                        val.astype(ref.dtype).reshape(blk_shape + (LANES,)))
        elif n % 2 == 0:
            held = results
        else:
            tile, v2 = n // GROUPS, (n % GROUPS) // 2
            rows = slice(2 * SUBLANES * v2, 2 * SUBLANES * (v2 + 1))
            pair = lambda a, b2: jnp.concatenate([a.reshape(DIL, SUBLANES, LANES),
                                                  b2.reshape(DIL, SUBLANES, LANES)], axis=1)
            for (ref, vals), (_, before) in zip(results, held):
                for j, val in enumerate(vals):
                    ref[tile, :, rows, j * LANES:(j + 1) * LANES] = pair(before[j], val).astype(ref.dtype)


def _banded_attention(q, k, v, bias, *, grid, blk, q_map, prev_blk, prev_map, units, shared_kv,
                      head_of, sinks=None, emit_stats=False, out_view=None, name):
    qw, kw = q.shape[-1], k.shape[-1]
    spec = lambda shape, w, imap: pl.BlockSpec(shape + (w,), imap)
    has_prev = prev_blk is not None
    in_specs, args = [spec(blk, qw, q_map)], [q]
    if has_prev:
        in_specs += [spec(prev_blk, kw, prev_map), spec(prev_blk, kw, prev_map)]
        args += [k, v]
    in_specs += [spec(blk, kw, q_map), spec(blk, kw, q_map),
                 pl.BlockSpec(bias.shape, lambda *g: (0, 0, 0, 0), pipeline_mode=pl.Buffered(1))]
    args += [k, v, bias]
    if sinks is not None:
        in_specs.append(pl.BlockSpec(memory_space=pltpu.SMEM))
        args.append(sinks)
    o_lead, o_blk, o_map = out_view if out_view is not None else (q.shape[:-1], blk, q_map)
    out_shape = [jax.ShapeDtypeStruct(o_lead + (qw,), _BF16)]
    out_specs = [spec(o_blk, qw, o_map)]
    if emit_stats:
        out_shape.append(jax.ShapeDtypeStruct(o_lead + (qw,), _F32))
        out_specs.append(spec(o_blk, qw, o_map))
    body = functools.partial(_attn_body, units=units, has_prev=has_prev, seq_axis=len(grid) - 1,
                             shared_kv=shared_kv, head_of=head_of, has_sink=sinks is not None,
                             emit_stats=emit_stats, regroup=out_view is not None)
    return pl.pallas_call(
        body, grid=grid, in_specs=in_specs, out_specs=out_specs, out_shape=out_shape,
        compiler_params=pltpu.CompilerParams(dimension_semantics=("parallel",) * len(grid),
                                             vmem_limit_bytes=VMEM_LIMIT),
        name=name,
    )(*args)


def _head_a(j, half):
    return j + 4 * half


def _head_b(j, half):
    return 2 * j + half


def _chain(n):
    return [((i,), (i - 1,) if i else None) for i in range(n)]


_GROUP_POS = DIL * (np.arange(BLOCK) % SUBLANES) + np.arange(BLOCK) // SUBLANES


def _token_order_attention(q, k, v, max_dist, b, s, **kw):
    nb, nt = s // BLOCK, s // TM
    bias = _band_bias(_GROUP_POS, np.concatenate([_GROUP_POS - BLOCK, _GROUP_POS]), max_dist, 1,
                      _alibi_slopes(Q_HEADS))
    view = lambda t: t.reshape(b, nb, BLOCK, t.shape[-1])
    tiles = NAT_BLOCKS // GROUPS
    outs = _banded_attention(
        view(q), view(k), view(v), bias, grid=(b, nb // NAT_BLOCKS),
        blk=(None, NAT_BLOCKS, BLOCK), q_map=lambda bi, n: (bi, n, 0, 0),
        prev_blk=(None, 1, BLOCK), prev_map=lambda bi, n: (bi, jnp.maximum(n * NAT_BLOCKS - 1, 0), 0, 0),
        units=_chain(NAT_BLOCKS),
        out_view=((b, nt, DIL, TI), (None, tiles, DIL, TI), lambda bi, n: (bi, n, 0, 0, 0)), **kw)
    return [o.reshape(b * s, o.shape[-1]) for o in outs]


def _mixer_b_dilated(qbp, kbp, vbp, b, s):
    slopes = _alibi_slopes(Q_HEADS)
    nt = s // TM
    common = dict(shared_kv=False, head_of=_head_b, emit_stats=True)
    flat = lambda t: t.reshape(b * s, t.shape[-1])
    w, r = B_PATTERNS[1]
    sub = (4 * np.arange(TI)[None, :] + np.arange(4)[:, None]).reshape(-1)
    bias = _band_bias(sub, np.concatenate([sub - BLOCK, sub]), w // r, r, slopes)
    view = lambda t: t.reshape(b, nt, 4, 4, TI, t.shape[-1])
    o4, st4 = _banded_attention(
        view(qbp), view(kbp), view(vbp), bias, grid=(b, 4, nt // D4_BLOCKS),
        blk=(None, D4_BLOCKS, 4, None, TI), q_map=lambda bi, c, n: (bi, n, 0, c, 0, 0),
        prev_blk=(None, 1, 4, None, TI),
        prev_map=lambda bi, c, n: (bi, jnp.maximum(n * D4_BLOCKS - 1, 0), 0, c, 0, 0),
        units=_chain(D4_BLOCKS), name="mixer_b_d4", **common)
    w, r = B_PATTERNS[2]
    bias = _band_bias(np.arange(BLOCK), np.arange(2 * BLOCK) - BLOCK, w // r, r, slopes)
    ng = nt // 4
    view = lambda t: t.reshape(b, ng, 4, DIL, TI, t.shape[-1])
    units = [((g, slice(None), c), (g - 1, slice(None), c) if g else None)
             for c in range(D16_RESIDUES) for g in range(ng)]
    o16, st16 = _banded_attention(
        view(qbp), view(kbp), view(vbp), bias, grid=(b, DIL // D16_RESIDUES),
        blk=(None, ng, 4, D16_RESIDUES, TI), q_map=lambda bi, c: (bi, 0, 0, c, 0, 0),
        prev_blk=None, prev_map=None, units=units, name="mixer_b_d16", **common)
    return (flat(o4), flat(st4)), (flat(o16), flat(st16))


def _layer_norm(y, g, b):
    mu = jnp.mean(y, axis=-1, keepdims=True)
    yc = y - mu
    var = jnp.mean(yc * yc, axis=-1, keepdims=True)
    return yc * lax.rsqrt(var + LN_EPS) * g + b


def _rms_norm(y, g):
    ms = jnp.mean(y * y, axis=-1, keepdims=True)
    return y * lax.rsqrt(ms + RMS_EPS) * g


def _mix_rows(rows, x_ref, oa_ref, o1_ref, st1_ref, o4_ref, st4_ref, o16_ref, st16_ref,
              wo_ref, ga_ref, gb_ref, lg_ref, lb_ref):
    lse1, lse4, lse16 = st1_ref[rows, :], st4_ref[rows, :], st16_ref[rows, :]
    mx = jnp.maximum(jnp.maximum(lse1, lse4), lse16)
    e1, e4, e16 = jnp.exp(lse1 - mx), jnp.exp(lse4 - mx), jnp.exp(lse16 - mx)
    merged = (e1 * o1_ref[rows, :].astype(_F32) + e4 * o4_ref[rows, :].astype(_F32)
              + e16 * o16_ref[rows, :].astype(_F32))
    ob = merged * (1.0 / (e1 + e4 + e16))
    obn = _rms_norm(ob, gb_ref[...]).astype(_BF16)
    oan = _rms_norm(oa_ref[rows, :].astype(_F32), ga_ref[...]).astype(_BF16)
    mix = _dot(oan, wo_ref[0:QW, :]) + _dot(obn, wo_ref[QW:2 * QW, :])
    return _layer_norm(ALPHA * x_ref[rows, :] + mix, lg_ref[...], lb_ref[...])


def _gelu_tanh(x):
    c = math.sqrt(2.0 / math.pi)
    return x * (0.5 * (1.0 + jnp.tanh(c * (x + 0.044715 * (x * x * x)))))


def _ffn_tile(h, keep, wg_ref, wv_ref, cg_ref, cv_ref, wd_ref, lg_ref, lb_ref, act_ref, tail_ref,
              between_chunks):
    hb = h.astype(_BF16)
    lo14, lo15 = (DIL - 2) * TI, (DIL - 1) * TI

    def shifted(tail8, chunk):
        return pltpu.roll(jnp.concatenate([tail8, chunk], axis=0), 1, 0)[SUBLANES:]

    for j in range(N_FF_CHUNKS):
        branches = []
        for which, (w_ref, c_ref) in enumerate(((wg_ref, cg_ref), (wv_ref, cv_ref))):
            u = _dot(hb, w_ref[j])
            slot = 2 * j + which
            tails = tail_ref[slot] * keep
            tail_ref[slot, 0:SUBLANES] = u[lo15 - SUBLANES:lo15]
            tail_ref[slot, SUBLANES:2 * SUBLANES] = u[TM - SUBLANES:TM]
            sh14 = shifted(tails[0:SUBLANES], u[lo14:lo15])
            sh15 = shifted(tails[SUBLANES:2 * SUBLANES], u[lo15:TM])
            prev1 = jnp.concatenate([sh15, u[:lo15]], axis=0)
            prev2 = jnp.concatenate([sh14, sh15, u[:lo14]], axis=0)
            cw = c_ref[j]
            branches.append(cw[2:3] * u + cw[1:2] * prev1 + cw[0:1] * prev2 + cw[3:4])
        act_ref[:, j * FF_CHUNK:(j + 1) * FF_CHUNK] = (_gelu_tanh(branches[0]) * branches[1]).astype(_BF16)
        between_chunks(j)
    ff = _dot(act_ref[...], wd_ref[...])
    return _layer_norm(ALPHA * h + ff, lg_ref[...], lb_ref[...])


def _mix_ffn_body(x_ref, oa_ref, o1_ref, st1_ref, o4_ref, st4_ref, o16_ref, st16_ref,
                  wo_ref, ga_ref, gb_ref, l1g_ref, l1b_ref,
                  wg_ref, wv_ref, cg_ref, cv_ref, wd_ref, l2g_ref, l2b_ref,
                  o_ref, hnext_ref, hcur_ref, act_ref, slab_ref, tail_ref, *, tiles_per_seq):
    i = pl.program_id(0)

    @pl.when(i == 0)
    def _():
        hnext_ref[...] = jnp.zeros_like(hnext_ref)
        tail_ref[...] = jnp.zeros_like(tail_ref)

    hcur_ref[...] = hnext_ref[...]

    def mix_piece(j):
        if j % MIX_EVERY == 0 and j // MIX_EVERY < MIX_PIECES:
            piece = j // MIX_EVERY
            rows = slice(piece * (TM // MIX_PIECES), (piece + 1) * (TM // MIX_PIECES))
            hnext_ref[rows, :] = _mix_rows(rows, x_ref, oa_ref, o1_ref, st1_ref, o4_ref, st4_ref,
                                           o16_ref, st16_ref, wo_ref, ga_ref, gb_ref, l1g_ref, l1b_ref)

    keep = ((i + tiles_per_seq - 1) % tiles_per_seq != 0).astype(_F32)
    out = _ffn_tile(hcur_ref[...], keep, wg_ref, wv_ref, cg_ref, cv_ref, wd_ref,
                    l2g_ref, l2b_ref, act_ref, tail_ref, mix_piece)
    n_slab = out.shape[-1] // LANES
    for j in range(n_slab):
        for c in range(DIL):
            slab_ref[j, pl.ds(c, TI, stride=DIL), :] = out[c * TI:(c + 1) * TI, j * LANES:(j + 1) * LANES]
    for j in range(n_slab):
        o_ref[:, j * LANES:(j + 1) * LANES] = slab_ref[j]


def _mix_ffn(xp, oa, o1, st1, o4, st4, o16, st16, wo_p, ga_p, gb, l1g, l1b, wg, wv, cg, cv, wd, l2g, l2b, s):
    m, d = xp.shape
    nt = m // TM
    row = lambda w: pl.BlockSpec((TM, w), lambda i: (jnp.minimum(i, nt - 1), 0))
    const = lambda a: pl.BlockSpec(a.shape, lambda i: (0,) * a.ndim, pipeline_mode=pl.Buffered(1))
    consts = (wo_p, ga_p, gb, l1g, l1b, wg, wv, cg, cv, wd, l2g, l2b)
    return pl.pallas_call(
        functools.partial(_mix_ffn_body, tiles_per_seq=s // TM),
        grid=(nt + 1,),
        in_specs=[row(d)] + [row(QW)] * 7 + [const(a) for a in consts],
        out_specs=pl.BlockSpec((TM, d), lambda i: (jnp.maximum(i - 1, 0), 0)),
        out_shape=jax.ShapeDtypeStruct((m, d), _F32),
        scratch_shapes=[pltpu.VMEM((TM, d), _F32), pltpu.VMEM((TM, d), _F32),
                        pltpu.VMEM((TM, D_FF), _BF16),
                        pltpu.VMEM((d // LANES, TM, LANES), _F32),
                        pltpu.VMEM((2 * N_FF_CHUNKS, 2 * SUBLANES, FF_CHUNK), _F32)],
        compiler_params=pltpu.CompilerParams(dimension_semantics=("arbitrary",),
                                             vmem_limit_bytes=MIX_FFN_VMEM_LIMIT),
        name="mix_ffn",
    )(xp, oa, o1, st1, o4, st4, o16, st16, *consts)


def _conv_table(conv_w, conv_b, lo):
    w = conv_w[:, lo:lo + D_FF].reshape(3, N_FF_CHUNKS, FF_CHUNK)
    bias = conv_b[lo:lo + D_FF].reshape(1, N_FF_CHUNKS, FF_CHUNK)
    t = jnp.concatenate([w, bias, jnp.zeros((4, N_FF_CHUNKS, FF_CHUNK), _F32)], axis=0)
    return t.transpose(1, 0, 2)


def kernel(x, w_in, norm_a_g, norm_b_g, sinks_a, w_o, ln1_g, ln1_b, w_up, conv_w, conv_b, w_down, ln2_g, ln2_b):
    b, s, d = x.shape
    assert s % (4 * TM) == 0 and (s // BLOCK) % NAT_BLOCKS == 0 and (s // TM) % D4_BLOCKS == 0
    assert d == 1024 and w_up.shape[1] == 2 * D_FF
    head_order = np.array([0, 4, 1, 5, 2, 6, 3, 7])
    a_cols = (head_order[:, None] * HEAD_DIM + np.arange(HEAD_DIM)[None, :]).reshape(-1)
    scale = 1.0 / math.sqrt(HEAD_DIM)
    off_b = QW + 2 * A_KV_W
    w_in_p = jnp.concatenate([w_in[:, a_cols] * scale, w_in[:, QW:off_b],
                              w_in[:, off_b:off_b + QW] * scale, w_in[:, off_b + QW:]], axis=1).astype(_BF16)
    wo_p = jnp.concatenate([w_o[a_cols], w_o[QW:]], axis=0).astype(_BF16)
    ga_p = norm_a_g[a_cols].reshape(1, QW)
    gb = norm_b_g.reshape(1, QW)
    chunked = lambda w: w.reshape(d, N_FF_CHUNKS, FF_CHUNK).transpose(1, 0, 2).astype(_BF16)
    wg, wv = chunked(w_up[:, :D_FF]), chunked(w_up[:, D_FF:])
    wd = w_down.astype(_BF16)
    cg, cv = _conv_table(conv_w, conv_b, 0), _conv_table(conv_w, conv_b, D_FF)

    xp, qa, ka, va, qb, kb, vb, qbp, kbp, vbp = _project(x.reshape(b * s, d), w_in_p)
    (oa,) = _token_order_attention(qa, ka, va, A_MAX_DIST, b, s, shared_kv=True, head_of=_head_a,
                                   sinks=sinks_a, name="mixer_a")
    w1, r1 = B_PATTERNS[0]
    o1, st1 = _token_order_attention(qb, kb, vb, w1 // r1, b, s, shared_kv=False, head_of=_head_b,
                                     emit_stats=True, name="mixer_b_d1")
    (o4, st4), (o16, st16) = _mixer_b_dilated(qbp, kbp, vbp, b, s)
    out = _mix_ffn(xp, oa, o1, st1, o4, st4, o16, st16, wo_p, ga_p, gb,
                   ln1_g.reshape(1, d), ln1_b.reshape(1, d),
                   wg, wv, cg, cv, wd, ln2_g.reshape(1, d), ln2_b.reshape(1, d), s)
    return out.reshape(b, s, d)
```

```python
import functools
import math

import numpy as np
import jax
import jax.numpy as jnp
from jax import lax
from jax.experimental import pallas as pl
from jax.experimental.pallas import tpu as pltpu

HEAD_DIM = 64
LANES = 128
SUBLANES = 8
Q_HEADS = 8
QW = Q_HEADS * HEAD_DIM
A_KV_W = 128
BLOCK = 128
A_MAX_DIST = 127
B_PATTERNS = ((128, 1), (512, 4), (2048, 16))
DIL = 16
TI = 32
TM = DIL * TI
GROUPS = TM // BLOCK
D_FF = 2816
FF_CHUNK = 256
N_FF_CHUNKS = D_FF // FF_CHUNK
MIX_PIECES = 4
MIX_EVERY = 2
ALPHA = 2.0 ** 0.25
LOG2E = math.log2(math.e)
LN_EPS = 1e-5
RMS_EPS = 1e-6
VMEM_LIMIT = 48 * 1024 * 1024
MIX_FFN_VMEM_LIMIT = 58 * 1024 * 1024
NAT_BLOCKS = 8
D4_BLOCKS = 4
D16_RESIDUES = 2

_BF16 = jnp.bfloat16
_F32 = jnp.float32


def _alibi_slopes(n):
    return np.array([2.0 ** (-8.0 * (i + 1) / n) for i in range(n)], dtype=np.float32)


def _dot(a, b):
    return jnp.dot(a, b, preferred_element_type=_F32)


def _dot_nt(a, b):
    return lax.dot_general(a, b, (((1,), (1,)), ((), ())), preferred_element_type=_F32)


def _group_order(y):
    pieces = [y[c * TI + SUBLANES * u:c * TI + SUBLANES * (u + 1)]
              for u in range(GROUPS) for c in range(DIL)]
    return jnp.concatenate(pieces, axis=0)


def _proj_body(*refs, n_x):
    x_refs, w_ref = refs[:n_x], refs[n_x]
    xp_ref, qa_ref, ka_ref, va_ref, qb_ref, kb_ref, vb_ref, qbp_ref, kbp_ref, vbp_ref = refs[n_x + 1:]
    cols = [jnp.concatenate([x_ref[pl.ds(c, TI, stride=DIL), :] for c in range(DIL)], axis=0)
            for x_ref in x_refs]
    xp = jnp.concatenate(cols, axis=1)
    xp_ref[...] = xp
    xb = xp.astype(_BF16)
    ya = _dot(xb, w_ref[:, 0:QW + 2 * A_KV_W])
    qa_ref[...] = _group_order(ya[:, 0:QW]).astype(_BF16)
    ka_ref[...] = _group_order(ya[:, QW:QW + A_KV_W]).astype(_BF16)
    va_ref[...] = _group_order(ya[:, QW + A_KV_W:QW + 2 * A_KV_W]).astype(_BF16)
    base = QW + 2 * A_KV_W
    for idx, (grp_ref, til_ref) in enumerate(((qb_ref, qbp_ref), (kb_ref, kbp_ref), (vb_ref, vbp_ref))):
        y = _dot(xb, w_ref[:, base + QW * idx:base + QW * (idx + 1)])
        til_ref[...] = y.astype(_BF16)
        grp_ref[...] = _group_order(y).astype(_BF16)


def _project(xm, w_in_p):
    m, d = xm.shape
    nt = m // TM
    n_x = d // LANES
    rows = lambda w, dt=_BF16: jax.ShapeDtypeStruct((m, w), dt)
    spec = lambda w: pl.BlockSpec((TM, w), lambda i: (i, 0))
    widths = (QW, A_KV_W, A_KV_W, QW, QW, QW, QW, QW, QW)
    return pl.pallas_call(
        functools.partial(_proj_body, n_x=n_x),
        grid=(nt,),
        in_specs=[pl.BlockSpec((TM, LANES), lambda i, j=j: (i, j)) for j in range(n_x)]
                 + [pl.BlockSpec(w_in_p.shape, lambda i: (0, 0), pipeline_mode=pl.Buffered(1))],
        out_specs=[spec(d)] + [spec(w) for w in widths],
        out_shape=[rows(d, _F32)] + [rows(w) for w in widths],
        compiler_params=pltpu.CompilerParams(dimension_semantics=("parallel",),
                                             vmem_limit_bytes=VMEM_LIMIT),
        name="in_proj",
    )(*([xm] * n_x), w_in_p)


def _band_bias(mq, mk, max_dist, unit, slopes):
    mq = jnp.asarray(mq, jnp.int32)[:, None]
    mk = jnp.asarray(mk, jnp.int32)[None, :]
    dist = mq - mk
    valid = (dist >= 0) & (dist <= max_dist)
    first = valid & (mk >= 0)
    pen = -LOG2E * (jnp.asarray(slopes)[:, None, None] * (dist * unit).astype(_F32)[None])
    neg = jnp.float32(-jnp.inf)
    return jnp.stack([jnp.where(first[None], pen, neg), jnp.where(valid[None], pen, neg)], 0)


def _attn_body(*refs, units, has_prev, seq_axis, shared_kv, head_of, has_sink, emit_stats, regroup):
    refs = list(refs)
    q_ref = refs.pop(0)
    kp_ref = refs.pop(0) if has_prev else None
    vp_ref = refs.pop(0) if has_prev else None
    kc_ref, vc_ref, bias_ref = refs.pop(0), refs.pop(0), refs.pop(0)
    sink_ref = refs.pop(0) if has_sink else None
    o_ref = refs.pop(0)
    st_ref = refs.pop(0) if emit_stats else None

    qw, kw = q_ref.shape[-1], kc_ref.shape[-1]
    low = lax.broadcasted_iota(jnp.int32, (BLOCK, LANES), 1) < HEAD_DIM
    zero = jnp.zeros((BLOCK, LANES), _BF16)
    blk_shape = q_ref[units[0][0]].shape[:-1]
    held = None

    for n, (cur, prev) in enumerate(units):
        q = q_ref[cur].reshape(BLOCK, qw)
        k_cur, v_cur = kc_ref[cur].reshape(BLOCK, kw), vc_ref[cur].reshape(BLOCK, kw)
        if prev is not None:
            k_prev, v_prev = kc_ref[prev].reshape(BLOCK, kw), vc_ref[prev].reshape(BLOCK, kw)
            sel = 1
        elif has_prev:
            k_prev, v_prev = kp_ref[0].reshape(BLOCK, kw), vp_ref[0].reshape(BLOCK, kw)
            sel = jnp.where(pl.program_id(seq_axis) == 0, 0, 1)
        else:
            k_prev, v_prev = k_cur, v_cur
            sel = 0
        k = jnp.concatenate([k_prev, k_cur], axis=0)
        v = jnp.concatenate([v_prev, v_cur], axis=0)
        outs, stats = [], []
        for j in range(qw // LANES):
            qj = q[:, j * LANES:(j + 1) * LANES]
            if shared_kv:
                kj, vj = k, v
            else:
                kj, vj = k[:, j * LANES:(j + 1) * LANES], v[:, j * LANES:(j + 1) * LANES]
            halves, lses = [], []
            for half in range(2):
                h = head_of(j, half)
                qm = jnp.where(low, qj, zero) if half == 0 else jnp.where(low, zero, qj)
                s = _dot_nt(qm, kj) + bias_ref[sel, h]
                m = jnp.max(s, axis=-1, keepdims=True)
                if has_sink:
                    sk = sink_ref[h]
                    m = jnp.maximum(m, sk)
                p = jnp.exp2(s - m)
                l = jnp.sum(p, axis=-1, keepdims=True)
                if has_sink:
                    l = l + jnp.exp2(sk - m)
                pv = _dot(p.astype(_BF16), vj)
                halves.append(pv * (1.0 / l))
                lses.append(m + jnp.log2(l))
            outs.append(jnp.where(low, halves[0], halves[1]))
            if emit_stats:
                stats.append(jnp.where(low, lses[0], lses[1]))
        results = [(o_ref, outs)] + ([(st_ref, stats)] if emit_stats else [])
        if not regroup:
            for ref, vals in results:
                for j, val in enumerate(vals):
                    ref[cur + (Ellipsis, slice(j * LANES, (j + 1) * LANES))] = (
                        val.astype(ref.dtype).reshape(blk_shape + (LANES,)))
        elif n % 2 == 0:
            held = results
        else:
            tile, v2 = n // GROUPS, (n % GROUPS) // 2
            rows = slice(2 * SUBLANES * v2, 2 * SUBLANES * (v2 + 1))
            pair = lambda a, b2: jnp.concatenate([a.reshape(DIL, SUBLANES, LANES),
                                                  b2.reshape(DIL, SUBLANES, LANES)], axis=1)
            for (ref, vals), (_, before) in zip(results, held):
                for j, val in enumerate(vals):
                    ref[tile, :, rows, j * LANES:(j + 1) * LANES] = pair(before[j], val).astype(ref.dtype)


def _banded_attention(q, k, v, bias, *, grid, blk, q_map, prev_blk, prev_map, units, shared_kv,
                      head_of, sinks=None, emit_stats=False, out_view=None, name):
    qw, kw = q.shape[-1], k.shape[-1]
    spec = lambda shape, w, imap: pl.BlockSpec(shape + (w,), imap)
    has_prev = prev_blk is not None
    in_specs, args = [spec(blk, qw, q_map)], [q]
    if has_prev:
        in_specs += [spec(prev_blk, kw, prev_map), spec(prev_blk, kw, prev_map)]
        args += [k, v]
    in_specs += [spec(blk, kw, q_map), spec(blk, kw, q_map),
                 pl.BlockSpec(bias.shape, lambda *g: (0, 0, 0, 0), pipeline_mode=pl.Buffered(1))]
    args += [k, v, bias]
    if sinks is not None:
        in_specs.append(pl.BlockSpec(memory_space=pltpu.SMEM))
        args.append(sinks)
    o_lead, o_blk, o_map = out_view if out_view is not None else (q.shape[:-1], blk, q_map)
    out_shape = [jax.ShapeDtypeStruct(o_lead + (qw,), _BF16)]
    out_specs = [spec(o_blk, qw, o_map)]
    if emit_stats:
        out_shape.append(jax.ShapeDtypeStruct(o_lead + (qw,), _F32))
        out_specs.append(spec(o_blk, qw, o_map))
    body = functools.partial(_attn_body, units=units, has_prev=has_prev, seq_axis=len(grid) - 1,
                             shared_kv=shared_kv, head_of=head_of, has_sink=sinks is not None,
                             emit_stats=emit_stats, regroup=out_view is not None)
    return pl.pallas_call(
        body, grid=grid, in_specs=in_specs, out_specs=out_specs, out_shape=out_shape,
        compiler_params=pltpu.CompilerParams(dimension_semantics=("parallel",) * len(grid),
                                             vmem_limit_bytes=VMEM_LIMIT),
        name=name,
    )(*args)


def _head_a(j, half):
    return j + 4 * half


def _head_b(j, half):
    return 2 * j + half


def _chain(n):
    return [((i,), (i - 1,) if i else None) for i in range(n)]


_GROUP_POS = DIL * (np.arange(BLOCK) % SUBLANES) + np.arange(BLOCK) // SUBLANES


def _token_order_attention(q, k, v, max_dist, b, s, **kw):
    nb, nt = s // BLOCK, s // TM
    bias = _band_bias(_GROUP_POS, np.concatenate([_GROUP_POS - BLOCK, _GROUP_POS]), max_dist, 1,
                      _alibi_slopes(Q_HEADS))
    view = lambda t: t.reshape(b, nb, BLOCK, t.shape[-1])
    tiles = NAT_BLOCKS // GROUPS
    outs = _banded_attention(
        view(q), view(k), view(v), bias, grid=(b, nb // NAT_BLOCKS),
        blk=(None, NAT_BLOCKS, BLOCK), q_map=lambda bi, n: (bi, n, 0, 0),
        prev_blk=(None, 1, BLOCK), prev_map=lambda bi, n: (bi, jnp.maximum(n * NAT_BLOCKS - 1, 0), 0, 0),
        units=_chain(NAT_BLOCKS),
        out_view=((b, nt, DIL, TI), (None, tiles, DIL, TI), lambda bi, n: (bi, n, 0, 0, 0)), **kw)
    return [o.reshape(b * s, o.shape[-1]) for o in outs]


def _mixer_b_dilated(qbp, kbp, vbp, b, s):
    slopes = _alibi_slopes(Q_HEADS)
    nt = s // TM
    common = dict(shared_kv=False, head_of=_head_b, emit_stats=True)
    flat = lambda t: t.reshape(b * s, t.shape[-1])
    w, r = B_PATTERNS[1]
    sub = (4 * np.arange(TI)[None, :] + np.arange(4)[:, None]).reshape(-1)
    bias = _band_bias(sub, np.concatenate([sub - BLOCK, sub]), w // r, r, slopes)
    view = lambda t: t.reshape(b, nt, 4, 4, TI, t.shape[-1])
    o4, st4 = _banded_attention(
        view(qbp), view(kbp), view(vbp), bias, grid=(b, 4, nt // D4_BLOCKS),
        blk=(None, D4_BLOCKS, 4, None, TI), q_map=lambda bi, c, n: (bi, n, 0, c, 0, 0),
        prev_blk=(None, 1, 4, None, TI),
        prev_map=lambda bi, c, n: (bi, jnp.maximum(n * D4_BLOCKS - 1, 0), 0, c, 0, 0),
        units=_chain(D4_BLOCKS), name="mixer_b_d4", **common)
    w, r = B_PATTERNS[2]
    bias = _band_bias(np.arange(BLOCK), np.arange(2 * BLOCK) - BLOCK, w // r, r, slopes)
    ng = nt // 4
    view = lambda t: t.reshape(b, ng, 4, DIL, TI, t.shape[-1])
    units = [((g, slice(None), c), (g - 1, slice(None), c) if g else None)
             for c in range(D16_RESIDUES) for g in range(ng)]
    o16, st16 = _banded_attention(
        view(qbp), view(kbp), view(vbp), bias, grid=(b, DIL // D16_RESIDUES),
        blk=(None, ng, 4, D16_RESIDUES, TI), q_map=lambda bi, c: (bi, 0, 0, c, 0, 0),
        prev_blk=None, prev_map=None, units=units, name="mixer_b_d16", **common)
    return (flat(o4), flat(st4)), (flat(o16), flat(st16))


def _layer_norm(y, g, b):
    mu = jnp.mean(y, axis=-1, keepdims=True)
    yc = y - mu
    var = jnp.mean(yc * yc, axis=-1, keepdims=True)
    return yc * lax.rsqrt(var + LN_EPS) * g + b


def _rms_norm(y, g):
    ms = jnp.mean(y * y, axis=-1, keepdims=True)
    return y * lax.rsqrt(ms + RMS_EPS) * g


def _mix_rows(rows, x_ref, oa_ref, o1_ref, st1_ref, o4_ref, st4_ref, o16_ref, st16_ref,
              wo_ref, ga_ref, gb_ref, lg_ref, lb_ref):
    lse1, lse4, lse16 = st1_ref[rows, :], st4_ref[rows, :], st16_ref[rows, :]
    mx = jnp.maximum(jnp.maximum(lse1, lse4), lse16)
    e1, e4, e16 = jnp.exp2(lse1 - mx), jnp.exp2(lse4 - mx), jnp.exp2(lse16 - mx)
    merged = (e1 * o1_ref[rows, :].astype(_F32) + e4 * o4_ref[rows, :].astype(_F32)
              + e16 * o16_ref[rows, :].astype(_F32))
    ob = merged * (1.0 / (e1 + e4 + e16))
    obn = _rms_norm(ob, gb_ref[...]).astype(_BF16)
    oan = _rms_norm(oa_ref[rows, :].astype(_F32), ga_ref[...]).astype(_BF16)
    mix = _dot(oan, wo_ref[0:QW, :]) + _dot(obn, wo_ref[QW:2 * QW, :])
    return _layer_norm(ALPHA * x_ref[rows, :] + mix, lg_ref[...], lb_ref[...])


def _gelu_tanh(x):
    c = math.sqrt(2.0 / math.pi)
    return x * (0.5 * (1.0 + jnp.tanh(c * (x + 0.044715 * (x * x * x)))))


def _ffn_tile(h, keep, wup_ref, conv_ref, wd_ref, lg_ref, lb_ref, act_ref, tail_ref, between_chunks):
    hb = h.astype(_BF16)
    lo14, lo15 = (DIL - 2) * TI, (DIL - 1) * TI

    def shifted(tail8, chunk):
        return pltpu.roll(jnp.concatenate([tail8, chunk], axis=0), 1, 0)[SUBLANES:]

    for j in range(N_FF_CHUNKS):
        branches = []
        for which in range(2):
            cols = slice(which * D_FF + j * FF_CHUNK, which * D_FF + (j + 1) * FF_CHUNK)
            u = _dot(hb, wup_ref[:, cols])
            slot = 2 * j + which
            tails = tail_ref[slot] * keep
            tail_ref[slot, 0:SUBLANES] = u[lo15 - SUBLANES:lo15]
            tail_ref[slot, SUBLANES:2 * SUBLANES] = u[TM - SUBLANES:TM]
            sh14 = shifted(tails[0:SUBLANES], u[lo14:lo15])
            sh15 = shifted(tails[SUBLANES:2 * SUBLANES], u[lo15:TM])
            prev1 = jnp.concatenate([sh15, u[:lo15]], axis=0)
            prev2 = jnp.concatenate([sh14, sh15, u[:lo14]], axis=0)
            cw = conv_ref[:, cols]
            branches.append(cw[2:3] * u + cw[1:2] * prev1 + cw[0:1] * prev2 + cw[3:4])
        act_ref[:, j * FF_CHUNK:(j + 1) * FF_CHUNK] = (_gelu_tanh(branches[0]) * branches[1]).astype(_BF16)
        between_chunks(j)
    ff = _dot(act_ref[...], wd_ref[...])
    return _layer_norm(ALPHA * h + ff, lg_ref[...], lb_ref[...])


def _mix_ffn_body(x_ref, oa_ref, o1_ref, st1_ref, o4_ref, st4_ref, o16_ref, st16_ref,
                  wo_ref, ga_ref, gb_ref, l1g_ref, l1b_ref,
                  wup_ref, conv_ref, wd_ref, l2g_ref, l2b_ref,
                  o_ref, hnext_ref, hcur_ref, act_ref, slab_ref, tail_ref, *, tiles_per_seq):
    i = pl.program_id(0)

    @pl.when(i == 0)
    def _():
        hnext_ref[...] = jnp.zeros_like(hnext_ref)
        tail_ref[...] = jnp.zeros_like(tail_ref)

    hcur_ref[...] = hnext_ref[...]

    def mix_piece(j):
        if j % MIX_EVERY == 0 and j // MIX_EVERY < MIX_PIECES:
            piece = j // MIX_EVERY
            rows = slice(piece * (TM // MIX_PIECES), (piece + 1) * (TM // MIX_PIECES))
            hnext_ref[rows, :] = _mix_rows(rows, x_ref, oa_ref, o1_ref, st1_ref, o4_ref, st4_ref,
                                           o16_ref, st16_ref, wo_ref, ga_ref, gb_ref, l1g_ref, l1b_ref)

    keep = ((i + tiles_per_seq - 1) % tiles_per_seq != 0).astype(_F32)
    out = _ffn_tile(hcur_ref[...], keep, wup_ref, conv_ref, wd_ref,
                    l2g_ref, l2b_ref, act_ref, tail_ref, mix_piece)
    n_slab = out.shape[-1] // LANES
    for j in range(n_slab):
        for c in range(DIL):
            slab_ref[j, pl.ds(c, TI, stride=DIL), :] = out[c * TI:(c + 1) * TI, j * LANES:(j + 1) * LANES]
    for j in range(n_slab):
        o_ref[:, j * LANES:(j + 1) * LANES] = slab_ref[j]


def _mix_ffn(xp, oa, o1, st1, o4, st4, o16, st16, wo_p, ga_p, gb, l1g, l1b, wup, conv, wd, l2g, l2b, s):
    m, d = xp.shape
    nt = m // TM
    row = lambda w: pl.BlockSpec((TM, w), lambda i: (jnp.minimum(i, nt - 1), 0))
    const = lambda a: pl.BlockSpec(a.shape, lambda i: (0,) * a.ndim, pipeline_mode=pl.Buffered(1))
    consts = (wo_p, ga_p, gb, l1g, l1b, wup, conv, wd, l2g, l2b)
    return pl.pallas_call(
        functools.partial(_mix_ffn_body, tiles_per_seq=s // TM),
        grid=(nt + 1,),
        in_specs=[row(d)] + [row(QW)] * 7 + [const(a) for a in consts],
        out_specs=pl.BlockSpec((TM, d), lambda i: (jnp.maximum(i - 1, 0), 0)),
        out_shape=jax.ShapeDtypeStruct((m, d), _F32),
        scratch_shapes=[pltpu.VMEM((TM, d), _F32), pltpu.VMEM((TM, d), _F32),
                        pltpu.VMEM((TM, D_FF), _BF16),
                        pltpu.VMEM((d // LANES, TM, LANES), _F32),
                        pltpu.VMEM((2 * N_FF_CHUNKS, 2 * SUBLANES, FF_CHUNK), _F32)],
        compiler_params=pltpu.CompilerParams(dimension_semantics=("arbitrary",),
                                             vmem_limit_bytes=MIX_FFN_VMEM_LIMIT),
        name="mix_ffn",
    )(xp, oa, o1, st1, o4, st4, o16, st16, *consts)


def _pair_heads(t, axis):
    shape = t.shape
    t = t.reshape(shape[:axis] + (2, 4, HEAD_DIM) + shape[axis + 1:])
    return jnp.swapaxes(t, axis, axis + 1).reshape(shape)


def kernel(x, w_in, norm_a_g, norm_b_g, sinks_a, w_o, ln1_g, ln1_b, w_up, conv_w, conv_b, w_down, ln2_g, ln2_b):
    b, s, d = x.shape
    assert s % (4 * TM) == 0 and (s // BLOCK) % NAT_BLOCKS == 0 and (s // TM) % D4_BLOCKS == 0
    assert d == 1024 and w_up.shape[1] == 2 * D_FF
    scale = LOG2E / math.sqrt(HEAD_DIM)
    off_b = QW + 2 * A_KV_W
    w_in_p = jnp.concatenate([_pair_heads(w_in[:, :QW], 1) * scale, w_in[:, QW:off_b],
                              w_in[:, off_b:off_b + QW] * scale, w_in[:, off_b + QW:]], axis=1).astype(_BF16)
    wo_p = jnp.concatenate([_pair_heads(w_o[:QW], 0), w_o[QW:]], axis=0).astype(_BF16)
    ga_p = _pair_heads(norm_a_g, 0).reshape(1, QW)
    gb = norm_b_g.reshape(1, QW)
    wup, wd = w_up.astype(_BF16), w_down.astype(_BF16)
    conv = jnp.concatenate([conv_w, conv_b[None], jnp.zeros((SUBLANES - 4, 2 * D_FF), _F32)], axis=0)

    xp, qa, ka, va, qb, kb, vb, qbp, kbp, vbp = _project(x.reshape(b * s, d), w_in_p)
    (oa,) = _token_order_attention(qa, ka, va, A_MAX_DIST, b, s, shared_kv=True, head_of=_head_a,
                                   sinks=sinks_a * LOG2E, name="mixer_a")
    w1, r1 = B_PATTERNS[0]
    o1, st1 = _token_order_attention(qb, kb, vb, w1 // r1, b, s, shared_kv=False, head_of=_head_b,
                                     emit_stats=True, name="mixer_b_d1")
    (o4, st4), (o16, st16) = _mixer_b_dilated(qbp, kbp, vbp, b, s)
    out = _mix_ffn(xp, oa, o1, st1, o4, st4, o16, st16, wo_p, ga_p, gb,
                   ln1_g.reshape(1, d), ln1_b.reshape(1, d),
                   wup, conv, wd, ln2_g.reshape(1, d), ln2_b.reshape(1, d), s)
    return out.reshape(b, s, d)
```

```python
import functools
import math

import numpy as np
import jax
import jax.numpy as jnp
from jax import lax
from jax.experimental import pallas as pl
from jax.experimental.pallas import tpu as pltpu

HEAD_DIM = 64
LANES = 128
SUBLANES = 8
Q_HEADS = 8
QW = Q_HEADS * HEAD_DIM
A_KV_W = 128
BLOCK = 128
A_MAX_DIST = 127
B_PATTERNS = ((128, 1), (512, 4), (2048, 16))
DIL = 16
TI = 32
TM = DIL * TI
GROUPS = TM // BLOCK
D_FF = 2816
FF_CHUNK = 256
N_FF_CHUNKS = D_FF // FF_CHUNK
MIX_PIECES = 4
MIX_EVERY = 2
ALPHA = 2.0 ** 0.25
LOG2E = math.log2(math.e)
LN_EPS = 1e-5
RMS_EPS = 1e-6
VMEM_LIMIT = 48 * 1024 * 1024
MIX_FFN_VMEM_LIMIT = 58 * 1024 * 1024
NAT_BLOCKS = 8
D4_BLOCKS = 4
D16_RESIDUES = 2

_BF16 = jnp.bfloat16
_F32 = jnp.float32


def _alibi_slopes(n):
    return np.array([2.0 ** (-8.0 * (i + 1) / n) for i in range(n)], dtype=np.float32)


def _dot(a, b):
    return jnp.dot(a, b, preferred_element_type=_F32)


def _dot_nt(a, b):
    return lax.dot_general(a, b, (((1,), (1,)), ((), ())), preferred_element_type=_F32)


def _group_order(y):
    pieces = [y[c * TI + SUBLANES * u:c * TI + SUBLANES * (u + 1)]
              for u in range(GROUPS) for c in range(DIL)]
    return jnp.concatenate(pieces, axis=0)


def _proj_body(*refs, n_x):
    x_refs, w_ref = refs[:n_x], refs[n_x]
    xp_ref, qa_ref, ka_ref, va_ref, qb_ref, kb_ref, vb_ref, qbp_ref, kbp_ref, vbp_ref = refs[n_x + 1:]
    cols = [jnp.concatenate([x_ref[pl.ds(c, TI, stride=DIL), :] for c in range(DIL)], axis=0)
            for x_ref in x_refs]
    xp = jnp.concatenate(cols, axis=1)
    xp_ref[...] = xp
    xb = xp.astype(_BF16)
    ya = _dot(xb, w_ref[:, 0:QW + 2 * A_KV_W])
    qa_ref[...] = _group_order(ya[:, 0:QW]).astype(_BF16)
    ka_ref[...] = _group_order(ya[:, QW:QW + A_KV_W]).astype(_BF16)
    va_ref[...] = _group_order(ya[:, QW + A_KV_W:QW + 2 * A_KV_W]).astype(_BF16)
    base = QW + 2 * A_KV_W
    for idx, (grp_ref, til_ref) in enumerate(((qb_ref, qbp_ref), (kb_ref, kbp_ref), (vb_ref, vbp_ref))):
        y = _dot(xb, w_ref[:, base + QW * idx:base + QW * (idx + 1)])
        til_ref[...] = y.astype(_BF16)
        grp_ref[...] = _group_order(y).astype(_BF16)


def _project(xm, w_in_p):
    m, d = xm.shape
    nt = m // TM
    n_x = d // LANES
    rows = lambda w, dt=_BF16: jax.ShapeDtypeStruct((m, w), dt)
    spec = lambda w: pl.BlockSpec((TM, w), lambda i: (i, 0))
    widths = (QW, A_KV_W, A_KV_W, QW, QW, QW, QW, QW, QW)
    return pl.pallas_call(
        functools.partial(_proj_body, n_x=n_x),
        grid=(nt,),
        in_specs=[pl.BlockSpec((TM, LANES), lambda i, j=j: (i, j)) for j in range(n_x)]
                 + [pl.BlockSpec(w_in_p.shape, lambda i: (0, 0), pipeline_mode=pl.Buffered(1))],
        out_specs=[spec(d)] + [spec(w) for w in widths],
        out_shape=[rows(d, _F32)] + [rows(w) for w in widths],
        compiler_params=pltpu.CompilerParams(dimension_semantics=("parallel",),
                                             vmem_limit_bytes=VMEM_LIMIT),
        name="in_proj",
    )(*([xm] * n_x), w_in_p)


def _band_bias(mq, mk, max_dist, unit, slopes):
    mq = np.asarray(mq, np.int32)[:, None]
    mk = np.asarray(mk, np.int32)[None, :]
    dist = mq - mk
    valid = (dist >= 0) & (dist <= max_dist)
    first = valid & (mk >= 0)
    pen = -np.float32(LOG2E) * (np.asarray(slopes)[:, None, None] * (dist * unit).astype(np.float32)[None])
    neg = np.float32(-np.inf)
    return np.stack([np.where(first[None], pen, neg), np.where(valid[None], pen, neg)], 0)


def _attn_body(*refs, units, has_prev, seq_axis, shared_kv, head_of, has_sink, emit_stats, regroup):
    refs = list(refs)
    q_ref = refs.pop(0)
    kp_ref = refs.pop(0) if has_prev else None
    vp_ref = refs.pop(0) if has_prev else None
    kc_ref, vc_ref, bias_ref = refs.pop(0), refs.pop(0), refs.pop(0)
    sink_ref = refs.pop(0) if has_sink else None
    o_ref = refs.pop(0)
    st_ref = refs.pop(0) if emit_stats else None

    qw, kw = q_ref.shape[-1], kc_ref.shape[-1]
    low = lax.broadcasted_iota(jnp.int32, (BLOCK, LANES), 1) < HEAD_DIM
    zero = jnp.zeros((BLOCK, LANES), _BF16)
    blk_shape = q_ref[units[0][0]].shape[:-1]
    held = None

    for n, (cur, prev) in enumerate(units):
        q = q_ref[cur].reshape(BLOCK, qw)
        k_cur, v_cur = kc_ref[cur].reshape(BLOCK, kw), vc_ref[cur].reshape(BLOCK, kw)
        if prev is not None:
            k_prev, v_prev = kc_ref[prev].reshape(BLOCK, kw), vc_ref[prev].reshape(BLOCK, kw)
            sel = 1
        elif has_prev:
            k_prev, v_prev = kp_ref[0].reshape(BLOCK, kw), vp_ref[0].reshape(BLOCK, kw)
            sel = jnp.where(pl.program_id(seq_axis) == 0, 0, 1)
        else:
            k_prev, v_prev = k_cur, v_cur
            sel = 0
        k = jnp.concatenate([k_prev, k_cur], axis=0)
        v = jnp.concatenate([v_prev, v_cur], axis=0)
        outs, stats = [], []
        for j in range(qw // LANES):
            qj = q[:, j * LANES:(j + 1) * LANES]
            if shared_kv:
                kj, vj = k, v
            else:
                kj, vj = k[:, j * LANES:(j + 1) * LANES], v[:, j * LANES:(j + 1) * LANES]
            pvs, ms, ls = [], [], []
            for half in range(2):
                h = head_of(j, half)
                qm = jnp.where(low, qj, zero) if half == 0 else jnp.where(low, zero, qj)
                s = _dot_nt(qm, kj) + bias_ref[sel, h]
                m = jnp.max(s, axis=-1, keepdims=True)
                if has_sink:
                    sk = sink_ref[h]
                    m = jnp.maximum(m, sk)
                p = jnp.exp2(s - m)
                l = jnp.sum(p, axis=-1, keepdims=True)
                if has_sink:
                    l = l + jnp.exp2(sk - m)
                pvs.append(_dot(p.astype(_BF16), vj))
                ms.append(m)
                ls.append(l)
            l2 = jnp.where(low, ls[0], ls[1])
            outs.append(jnp.where(low, pvs[0], pvs[1]) * (1.0 / l2))
            if emit_stats:
                stats.append(jnp.where(low, ms[0], ms[1]) + jnp.log2(l2))
        results = [(o_ref, outs)] + ([(st_ref, stats)] if emit_stats else [])
        if not regroup:
            for ref, vals in results:
                for j, val in enumerate(vals):
                    ref[cur + (Ellipsis, slice(j * LANES, (j + 1) * LANES))] = (
                        val.astype(ref.dtype).reshape(blk_shape + (LANES,)))
        elif n % 2 == 0:
            held = results
        else:
            tile, v2 = n // GROUPS, (n % GROUPS) // 2
            rows = slice(2 * SUBLANES * v2, 2 * SUBLANES * (v2 + 1))
            pair = lambda a, b2: jnp.concatenate([a.reshape(DIL, SUBLANES, LANES),
                                                  b2.reshape(DIL, SUBLANES, LANES)], axis=1)
            for (ref, vals), (_, before) in zip(results, held):
                for j, val in enumerate(vals):
                    ref[tile, :, rows, j * LANES:(j + 1) * LANES] = pair(before[j], val).astype(ref.dtype)


def _banded_attention(q, k, v, bias, *, grid, blk, q_map, prev_blk, prev_map, units, shared_kv,
                      head_of, sinks=None, emit_stats=False, out_view=None, name):
    qw, kw = q.shape[-1], k.shape[-1]
    spec = lambda shape, w, imap: pl.BlockSpec(shape + (w,), imap)
    has_prev = prev_blk is not None
    in_specs, args = [spec(blk, qw, q_map)], [q]
    if has_prev:
        in_specs += [spec(prev_blk, kw, prev_map), spec(prev_blk, kw, prev_map)]
        args += [k, v]
    in_specs += [spec(blk, kw, q_map), spec(blk, kw, q_map),
                 pl.BlockSpec(bias.shape, lambda *g: (0, 0, 0, 0), pipeline_mode=pl.Buffered(1))]
    args += [k, v, bias]
    if sinks is not None:
        in_specs.append(pl.BlockSpec(memory_space=pltpu.SMEM))
        args.append(sinks)
    o_lead, o_blk, o_map = out_view if out_view is not None else (q.shape[:-1], blk, q_map)
    out_shape = [jax.ShapeDtypeStruct(o_lead + (qw,), _BF16)]
    out_specs = [spec(o_blk, qw, o_map)]
    if emit_stats:
        out_shape.append(jax.ShapeDtypeStruct(o_lead + (qw,), _F32))
        out_specs.append(spec(o_blk, qw, o_map))
    body = functools.partial(_attn_body, units=units, has_prev=has_prev, seq_axis=len(grid) - 1,
                             shared_kv=shared_kv, head_of=head_of, has_sink=sinks is not None,
                             emit_stats=emit_stats, regroup=out_view is not None)
    return pl.pallas_call(
        body, grid=grid, in_specs=in_specs, out_specs=out_specs, out_shape=out_shape,
        compiler_params=pltpu.CompilerParams(dimension_semantics=("parallel",) * len(grid),
                                             vmem_limit_bytes=VMEM_LIMIT),
        name=name,
    )(*args)


def _head_a(j, half):
    return j + 4 * half


def _head_b(j, half):
    return 2 * j + half


def _chain(n):
    return [((i,), (i - 1,) if i else None) for i in range(n)]


_GROUP_POS = DIL * (np.arange(BLOCK) % SUBLANES) + np.arange(BLOCK) // SUBLANES


def _token_order_attention(q, k, v, max_dist, b, s, **kw):
    nb, nt = s // BLOCK, s // TM
    bias = _band_bias(_GROUP_POS, np.concatenate([_GROUP_POS - BLOCK, _GROUP_POS]), max_dist, 1,
                      _alibi_slopes(Q_HEADS))
    view = lambda t: t.reshape(b, nb, BLOCK, t.shape[-1])
    tiles = NAT_BLOCKS // GROUPS
    outs = _banded_attention(
        view(q), view(k), view(v), bias, grid=(b, nb // NAT_BLOCKS),
        blk=(None, NAT_BLOCKS, BLOCK), q_map=lambda bi, n: (bi, n, 0, 0),
        prev_blk=(None, 1, BLOCK), prev_map=lambda bi, n: (bi, jnp.maximum(n * NAT_BLOCKS - 1, 0), 0, 0),
        units=_chain(NAT_BLOCKS),
        out_view=((b, nt, DIL, TI), (None, tiles, DIL, TI), lambda bi, n: (bi, n, 0, 0, 0)), **kw)
    return [o.reshape(b * s, o.shape[-1]) for o in outs]


def _mixer_b_dilated(qbp, kbp, vbp, b, s):
    slopes = _alibi_slopes(Q_HEADS)
    nt = s // TM
    common = dict(shared_kv=False, head_of=_head_b, emit_stats=True)
    flat = lambda t: t.reshape(b * s, t.shape[-1])
    w, r = B_PATTERNS[1]
    sub = (4 * np.arange(TI)[None, :] + np.arange(4)[:, None]).reshape(-1)
    bias = _band_bias(sub, np.concatenate([sub - BLOCK, sub]), w // r, r, slopes)
    view = lambda t: t.reshape(b, nt, 4, 4, TI, t.shape[-1])
    o4, st4 = _banded_attention(
        view(qbp), view(kbp), view(vbp), bias, grid=(b, 4, nt // D4_BLOCKS),
        blk=(None, D4_BLOCKS, 4, None, TI), q_map=lambda bi, c, n: (bi, n, 0, c, 0, 0),
        prev_blk=(None, 1, 4, None, TI),
        prev_map=lambda bi, c, n: (bi, jnp.maximum(n * D4_BLOCKS - 1, 0), 0, c, 0, 0),
        units=_chain(D4_BLOCKS), name="mixer_b_d4", **common)
    w, r = B_PATTERNS[2]
    bias = _band_bias(np.arange(BLOCK), np.arange(2 * BLOCK) - BLOCK, w // r, r, slopes)
    ng = nt // 4
    view = lambda t: t.reshape(b, ng, 4, DIL, TI, t.shape[-1])
    units = [((g, slice(None), c), (g - 1, slice(None), c) if g else None)
             for c in range(D16_RESIDUES) for g in range(ng)]
    o16, st16 = _banded_attention(
        view(qbp), view(kbp), view(vbp), bias, grid=(b, DIL // D16_RESIDUES),
        blk=(None, ng, 4, D16_RESIDUES, TI), q_map=lambda bi, c: (bi, 0, 0, c, 0, 0),
        prev_blk=None, prev_map=None, units=units, name="mixer_b_d16", **common)
    return (flat(o4), flat(st4)), (flat(o16), flat(st16))


def _layer_norm(y, g, b):
    mu = jnp.mean(y, axis=-1, keepdims=True)
    yc = y - mu
    var = jnp.mean(yc * yc, axis=-1, keepdims=True)
    return yc * lax.rsqrt(var + LN_EPS) * g + b


def _rms_norm(y, g):
    ms = jnp.mean(y * y, axis=-1, keepdims=True)
    return y * lax.rsqrt(ms + RMS_EPS) * g


def _mix_rows(rows, x_ref, oa_ref, o1_ref, st1_ref, o4_ref, st4_ref, o16_ref, st16_ref,
              wo_ref, ga_ref, gb_ref, lg_ref, lb_ref):
    lse1, lse4, lse16 = st1_ref[rows, :], st4_ref[rows, :], st16_ref[rows, :]
    mx = jnp.maximum(jnp.maximum(lse1, lse4), lse16)
    e1, e4, e16 = jnp.exp2(lse1 - mx), jnp.exp2(lse4 - mx), jnp.exp2(lse16 - mx)
    merged = (e1 * o1_ref[rows, :].astype(_F32) + e4 * o4_ref[rows, :].astype(_F32)
              + e16 * o16_ref[rows, :].astype(_F32))
    ob = merged * (1.0 / (e1 + e4 + e16))
    obn = _rms_norm(ob, gb_ref[...]).astype(_BF16)
    oan = _rms_norm(oa_ref[rows, :].astype(_F32), ga_ref[...]).astype(_BF16)
    mix = _dot(oan, wo_ref[0:QW, :]) + _dot(obn, wo_ref[QW:2 * QW, :])
    return _layer_norm(ALPHA * x_ref[rows, :] + mix, lg_ref[...], lb_ref[...])


def _gelu_tanh(x):
    c = math.sqrt(2.0 / math.pi)
    return x * (0.5 * (1.0 + jnp.tanh(c * (x + 0.044715 * (x * x * x)))))


def _ffn_tile(h, keep, wup_ref, conv_ref, wd_ref, lg_ref, lb_ref, act_ref, tail_ref, between_chunks):
    hb = h.astype(_BF16)
    lo14, lo15 = (DIL - 2) * TI, (DIL - 1) * TI

    def shifted(tail8, chunk):
        return pltpu.roll(jnp.concatenate([tail8, chunk], axis=0), 1, 0)[SUBLANES:]

    for j in range(N_FF_CHUNKS):
        branches = []
        for which in range(2):
            cols = slice(which * D_FF + j * FF_CHUNK, which * D_FF + (j + 1) * FF_CHUNK)
            u = _dot(hb, wup_ref[:, cols])
            slot = 2 * j + which
            tails = tail_ref[slot] * keep
            tail_ref[slot, 0:SUBLANES] = u[lo15 - SUBLANES:lo15]
            tail_ref[slot, SUBLANES:2 * SUBLANES] = u[TM - SUBLANES:TM]
            sh14 = shifted(tails[0:SUBLANES], u[lo14:lo15])
            sh15 = shifted(tails[SUBLANES:2 * SUBLANES], u[lo15:TM])
            prev1 = jnp.concatenate([sh15, u[:lo15]], axis=0)
            prev2 = jnp.concatenate([sh14, sh15, u[:lo14]], axis=0)
            cw = conv_ref[:, cols]
            branches.append(cw[2:3] * u + cw[1:2] * prev1 + cw[0:1] * prev2 + cw[3:4])
        act_ref[:, j * FF_CHUNK:(j + 1) * FF_CHUNK] = (_gelu_tanh(branches[0]) * branches[1]).astype(_BF16)
        between_chunks(j)
    ff = _dot(act_ref[...], wd_ref[...])
    return _layer_norm(ALPHA * h + ff, lg_ref[...], lb_ref[...])


def _mix_ffn_body(x_ref, oa_ref, o1_ref, st1_ref, o4_ref, st4_ref, o16_ref, st16_ref,
                  wo_ref, ga_ref, gb_ref, l1g_ref, l1b_ref,
                  wup_ref, conv_ref, wd_ref, l2g_ref, l2b_ref,
                  o_ref, hnext_ref, hcur_ref, act_ref, slab_ref, tail_ref, *, tiles_per_seq):
    i = pl.program_id(0)

    @pl.when(i == 0)
    def _():
        hnext_ref[...] = jnp.zeros_like(hnext_ref)
        tail_ref[...] = jnp.zeros_like(tail_ref)

    hcur_ref[...] = hnext_ref[...]

    def mix_piece(j):
        if j % MIX_EVERY == 0 and j // MIX_EVERY < MIX_PIECES:
            piece = j // MIX_EVERY
            rows = slice(piece * (TM // MIX_PIECES), (piece + 1) * (TM // MIX_PIECES))
            hnext_ref[rows, :] = _mix_rows(rows, x_ref, oa_ref, o1_ref, st1_ref, o4_ref, st4_ref,
                                           o16_ref, st16_ref, wo_ref, ga_ref, gb_ref, l1g_ref, l1b_ref)

    keep = ((i + tiles_per_seq - 1) % tiles_per_seq != 0).astype(_F32)
    out = _ffn_tile(hcur_ref[...], keep, wup_ref, conv_ref, wd_ref,
                    l2g_ref, l2b_ref, act_ref, tail_ref, mix_piece)
    n_slab = out.shape[-1] // LANES
    for j in range(n_slab):
        for c in range(DIL):
            slab_ref[j, pl.ds(c, TI, stride=DIL), :] = out[c * TI:(c + 1) * TI, j * LANES:(j + 1) * LANES]
    for j in range(n_slab):
        o_ref[:, j * LANES:(j + 1) * LANES] = slab_ref[j]


def _mix_ffn(xp, oa, o1, st1, o4, st4, o16, st16, wo_p, ga_p, gb, l1g, l1b, wup, conv, wd, l2g, l2b, s):
    m, d = xp.shape
    nt = m // TM
    row = lambda w: pl.BlockSpec((TM, w), lambda i: (jnp.minimum(i, nt - 1), 0))
    const = lambda a: pl.BlockSpec(a.shape, lambda i: (0,) * a.ndim, pipeline_mode=pl.Buffered(1))
    consts = (wo_p, ga_p, gb, l1g, l1b, wup, conv, wd, l2g, l2b)
    return pl.pallas_call(
        functools.partial(_mix_ffn_body, tiles_per_seq=s // TM),
        grid=(nt + 1,),
        in_specs=[row(d)] + [row(QW)] * 7 + [const(a) for a in consts],
        out_specs=pl.BlockSpec((TM, d), lambda i: (jnp.maximum(i - 1, 0), 0)),
        out_shape=jax.ShapeDtypeStruct((m, d), _F32),
        scratch_shapes=[pltpu.VMEM((TM, d), _F32), pltpu.VMEM((TM, d), _F32),
                        pltpu.VMEM((TM, D_FF), _BF16),
                        pltpu.VMEM((d // LANES, TM, LANES), _F32),
                        pltpu.VMEM((2 * N_FF_CHUNKS, 2 * SUBLANES, FF_CHUNK), _F32)],
        compiler_params=pltpu.CompilerParams(dimension_semantics=("arbitrary",),
                                             vmem_limit_bytes=MIX_FFN_VMEM_LIMIT),
        name="mix_ffn",
    )(xp, oa, o1, st1, o4, st4, o16, st16, *consts)


def _pair_heads(t, axis):
    shape = t.shape
    t = t.reshape(shape[:axis] + (2, 4, HEAD_DIM) + shape[axis + 1:])
    return jnp.swapaxes(t, axis, axis + 1).reshape(shape)


def kernel(x, w_in, norm_a_g, norm_b_g, sinks_a, w_o, ln1_g, ln1_b, w_up, conv_w, conv_b, w_down, ln2_g, ln2_b):
    b, s, d = x.shape
    assert s % (4 * TM) == 0 and (s // BLOCK) % NAT_BLOCKS == 0 and (s // TM) % D4_BLOCKS == 0
    assert d == 1024 and w_up.shape[1] == 2 * D_FF
    scale = LOG2E / math.sqrt(HEAD_DIM)
    off_b = QW + 2 * A_KV_W
    w_in_p = jnp.concatenate([_pair_heads(w_in[:, :QW], 1) * scale, w_in[:, QW:off_b],
                              w_in[:, off_b:off_b + QW] * scale, w_in[:, off_b + QW:]], axis=1).astype(_BF16)
    wo_p = jnp.concatenate([_pair_heads(w_o[:QW], 0), w_o[QW:]], axis=0).astype(_BF16)
    ga_p = _pair_heads(norm_a_g, 0).reshape(1, QW)
    gb = norm_b_g.reshape(1, QW)
    wup, wd = w_up.astype(_BF16), w_down.astype(_BF16)
    conv = jnp.concatenate([conv_w, conv_b[None], jnp.zeros((SUBLANES - 4, 2 * D_FF), _F32)], axis=0)

    xp, qa, ka, va, qb, kb, vb, qbp, kbp, vbp = _project(x.reshape(b * s, d), w_in_p)
    (oa,) = _token_order_attention(qa, ka, va, A_MAX_DIST, b, s, shared_kv=True, head_of=_head_a,
                                   sinks=sinks_a * LOG2E, name="mixer_a")
    w1, r1 = B_PATTERNS[0]
    o1, st1 = _token_order_attention(qb, kb, vb, w1 // r1, b, s, shared_kv=False, head_of=_head_b,
                                     emit_stats=True, name="mixer_b_d1")
    (o4, st4), (o16, st16) = _mixer_b_dilated(qbp, kbp, vbp, b, s)
    out = _mix_ffn(xp, oa, o1, st1, o4, st4, o16, st16, wo_p, ga_p, gb,
                   ln1_g.reshape(1, d), ln1_b.reshape(1, d),
                   wup, conv, wd, ln2_g.reshape(1, d), ln2_b.reshape(1, d), s)
    return out.reshape(b, s, d)
```
